```python
import math
import jax, jax.numpy as jnp
from jax import lax
import numpy as np

D_MODEL = 1024
BATCH = 8
SEQ = 4096
DEPTH = 4

N_A_LAYERS = DEPTH // 2
N_B_LAYERS = DEPTH - N_A_LAYERS
D_FF = 4 * D_MODEL
NORM_EPS = 1e-6
NEG = -1e30

DIFF_HEAD_DIM = 64
DIFF_HEADS = D_MODEL // (2 * DIFF_HEAD_DIM)
Q_BLOCK = 128

REL_BUCKETS = 32
REL_MAX_DIST = 128
REL_HEADS = 2 * DIFF_HEADS

NSA_HEAD_DIM = 64
NSA_HEADS = D_MODEL // NSA_HEAD_DIM
NSA_KV_HEADS = 4
NSA_GROUP = NSA_HEADS // NSA_KV_HEADS
CMP_LEN = 32
CMP_STRIDE = 16
CMP_HIDDEN = 256
SLC_LEN = 64
SLC_TOPK = 16
SLC_FORCED_LOCAL = 2
FORCE_BONUS = 1e4
WINDOW = 512
NSA_Q_BLOCK = 32
NSA_IN_COLS = NSA_HEADS * NSA_HEAD_DIM + 3 * NSA_HEADS
KV_COLS = 6 * NSA_KV_HEADS * NSA_HEAD_DIM

kernel_name = "yoco_diffattn_nsa_hybrid"


def rms_norm(x, g):
    xf = x.astype(jnp.float32)
    y = xf * lax.rsqrt(jnp.mean(xf * xf, axis=-1, keepdims=True) + NORM_EPS)
    return (y * g.astype(jnp.float32)).astype(x.dtype)


def modulate(h, shift, scale):
    return h * (1 + scale) + shift


def t5_bucket(dist):
    n = jnp.maximum(dist, 0)
    max_exact = REL_BUCKETS // 2
    nf = jnp.maximum(n, 1).astype(jnp.float32)
    large = max_exact + (jnp.log(nf / max_exact) / math.log(REL_MAX_DIST / max_exact)
                         * (REL_BUCKETS - max_exact)).astype(jnp.int32)
    large = jnp.minimum(large, REL_BUCKETS - 1)
    return jnp.where(n < max_exact, n, large)


def squared_relu_mlp(h, w1, w2):
    return jnp.square(jax.nn.relu(h @ w1)) @ w2


def diff_attention(h, w_in, w_out, lam, subln, rel_bias, layer_idx):
    B, T, _ = h.shape
    H, d = DIFF_HEADS, DIFF_HEAD_DIM
    q, k, v = jnp.split(h @ w_in, 3, axis=-1)
    q = q.reshape(B, T, H, 2, d)
    k = k.reshape(B, T, H, 2, d)
    v = v.reshape(B, T, H, 2 * d)
    lam_init = 0.8 - 0.6 * math.exp(-0.3 * layer_idx)
    lf = lam.astype(jnp.float32)
    lam_full = jnp.exp(jnp.sum(lf[0] * lf[1])) - jnp.exp(jnp.sum(lf[2] * lf[3])) + lam_init
    bias_tab = rel_bias.reshape(REL_BUCKETS, H, 2).astype(jnp.float32)
    k_pos = jnp.arange(T)
    scale = d ** -0.5
    nblk = T // Q_BLOCK
    qb = q.reshape(B, nblk, Q_BLOCK, H, 2, d).transpose(1, 0, 2, 3, 4, 5)

    def block(args):
        qi, blk = args
        q_pos = blk * Q_BLOCK + jnp.arange(Q_BLOCK)
        dist = q_pos[:, None] - k_pos[None, :]
        bias = bias_tab[t5_bucket(dist)].transpose(2, 3, 0, 1)
        s = jnp.einsum('bqhmd,bkhmd->bhmqk', qi, k,
                       preferred_element_type=jnp.float32) * scale + bias[None]
        s = jnp.where((dist >= 0)[None, None, None], s, NEG)
        p = jax.nn.softmax(s, axis=-1)
        a = p[:, :, 0] - lam_full * p[:, :, 1]
        return jnp.einsum('bhqk,bkhe->bqhe', a.astype(v.dtype), v)

    o = lax.map(block, (qb, jnp.arange(nblk)))
    o = o.transpose(1, 0, 2, 3, 4).reshape(B, T, H, 2 * d)
    o = rms_norm(o, subln) * (1 - lam_init)
    return o.reshape(B, T, H * 2 * d) @ w_out


def nsa_shared_kv(h, w_kv, cmp_pos, cmp_w1, cmp_w2):
    B, T, _ = h.shape
    G, d = NSA_KV_HEADS, NSA_HEAD_DIM
    kv = (h @ w_kv).reshape(B, T, 6, G, d).transpose(2, 0, 3, 1, 4)
    n_cmp = (T - CMP_LEN) // CMP_STRIDE + 1
    idx = jnp.arange(n_cmp)[:, None] * CMP_STRIDE + jnp.arange(CMP_LEN)[None, :]
    blocks = kv[0:2][:, :, :, idx] + cmp_pos[:, None, None, None]
    flat = blocks.reshape(2, B, G, n_cmp, CMP_LEN * d)
    hid = jax.nn.gelu(jnp.einsum('sbgnf,sfh->sbgnh', flat, cmp_w1))
    cmp = jnp.einsum('sbgnh,shd->sbgnd', hid, cmp_w2)
    return (cmp[0], cmp[1], kv[2], kv[3], kv[4], kv[5])


def nsa_attention(h, w_in, w_out, rel_bias, k_cmp, v_cmp, k_slc, v_slc, k_win, v_win):
    B, T, _ = h.shape
    H, G, R, d = NSA_HEADS, NSA_KV_HEADS, NSA_GROUP, NSA_HEAD_DIM
    proj = h @ w_in
    q = proj[..., :H * d].reshape(B, T, G, R, d)
    gates = jax.nn.sigmoid(proj[..., H * d:].astype(jnp.float32)).reshape(B, T, G, R, 3)
    scale = d ** -0.5
    bias_tab = rel_bias.reshape(REL_BUCKETS, G, R).astype(jnp.float32)
    bias_tab_g = bias_tab.transpose(1, 0, 2)
    n_cmp = k_cmp.shape[2]
    n_slc = T // SLC_LEN
    top_k = min(SLC_TOPK, n_slc)
    cmp_start = jnp.arange(n_cmp) * CMP_STRIDE
    cmp_end = cmp_start + CMP_LEN - 1
    slc_start = jnp.arange(n_slc) * SLC_LEN
    overlap = ((cmp_start[:, None] < slc_start[None, :] + SLC_LEN)
               & (cmp_end[:, None] >= slc_start[None, :])).astype(jnp.float32)
    k_slc_blk = k_slc.reshape(B, G, n_slc, SLC_LEN, d)
    v_slc_blk = v_slc.reshape(B, G, n_slc, SLC_LEN, d)
    k_win_pad = jnp.pad(k_win, ((0, 0), (0, 0), (WINDOW, 0), (0, 0)))
    v_win_pad = jnp.pad(v_win, ((0, 0), (0, 0), (WINDOW, 0), (0, 0)))
    b_idx = jnp.arange(B)[:, None, None, None]
    g_idx = jnp.arange(G)[None, :, None, None]
    j_idx = jnp.arange(n_slc)
    in_offs = jnp.arange(SLC_LEN)
    nblk = T // NSA_Q_BLOCK
    qb = q.reshape(B, nblk, NSA_Q_BLOCK, G, R, d).transpose(1, 0, 2, 3, 4, 5)
    gb = gates.reshape(B, nblk, NSA_Q_BLOCK, G, R, 3).transpose(1, 0, 2, 3, 4, 5)

    def block(args):
        qi, gi, blk = args
        q_pos = blk * NSA_Q_BLOCK + jnp.arange(NSA_Q_BLOCK)
        s_c = jnp.einsum('bqgrd,bgnd->bgrqn', qi, k_cmp,
                         preferred_element_type=jnp.float32) * scale
        valid_c = cmp_end[None, :] <= q_pos[:, None]
        p_c = jax.nn.softmax(jnp.where(valid_c, s_c, NEG), axis=-1) * valid_c
        o_c = jnp.einsum('bgrqn,bgnd->bqgrd', p_c.astype(v_cmp.dtype), v_cmp)
        imp = jnp.einsum('bgrqn,nj->bgqj', p_c, overlap)
        q_blk = q_pos // SLC_LEN
        forced = (j_idx[None, :] == 0) | ((j_idx[None, :] <= q_blk[:, None])
                                          & (j_idx[None, :] > q_blk[:, None] - SLC_FORCED_LOCAL))
        imp = jnp.where(forced[None, None], FORCE_BONUS, imp)
        imp = jnp.where((j_idx[None, :] <= q_blk[:, None])[None, None], imp, NEG)
        _, sel = lax.top_k(imp, top_k)
        ks = k_slc_blk[b_idx, g_idx, sel].reshape(B, G, NSA_Q_BLOCK, top_k * SLC_LEN, d)
        vs = v_slc_blk[b_idx, g_idx, sel].reshape(B, G, NSA_Q_BLOCK, top_k * SLC_LEN, d)
        pos_s = (sel[..., None] * SLC_LEN + in_offs).reshape(B, G, NSA_Q_BLOCK, top_k * SLC_LEN)
        dist_s = q_pos[None, None, :, None] - pos_s
        bias_s = bias_tab_g[g_idx, t5_bucket(dist_s)].transpose(0, 1, 4, 2, 3)
        s_s = jnp.einsum('bqgrd,bgqkd->bgrqk', qi, ks,
                         preferred_element_type=jnp.float32) * scale + bias_s
        s_s = jnp.where((dist_s >= 0)[:, :, None], s_s, NEG)
        p_s = jax.nn.softmax(s_s, axis=-1)
        o_s = jnp.einsum('bgrqk,bgqkd->bqgrd', p_s.astype(vs.dtype), vs)
        start = blk * NSA_Q_BLOCK
        kw = lax.dynamic_slice_in_dim(k_win_pad, start, WINDOW + NSA_Q_BLOCK, axis=2)
        vw = lax.dynamic_slice_in_dim(v_win_pad, start, WINDOW + NSA_Q_BLOCK, axis=2)
        k_pos = start - WINDOW + jnp.arange(WINDOW + NSA_Q_BLOCK)
        dist_w = q_pos[:, None] - k_pos[None, :]
        valid_w = (dist_w >= 0) & (dist_w < WINDOW) & (k_pos[None, :] >= 0)
        bias_w = bias_tab[t5_bucket(dist_w)].transpose(2, 3, 0, 1)
        s_w = jnp.einsum('bqgrd,bgkd->bgrqk', qi, kw,
                         preferred_element_type=jnp.float32) * scale + bias_w[None]
        p_w = jax.nn.softmax(jnp.where(valid_w[None, None, None], s_w, NEG), axis=-1)
        o_w = jnp.einsum('bgrqk,bgkd->bqgrd', p_w.astype(vw.dtype), vw)
        o = gi[..., 0:1] * o_c + gi[..., 1:2] * o_s + gi[..., 2:3] * o_w
        return o.astype(qi.dtype)

    o = lax.map(block, (qb, gb, jnp.arange(nblk)))
    o = o.transpose(1, 0, 2, 3, 4, 5).reshape(B, T, H * d)
    return o @ w_out


def setup_inputs(seed: int = 0) -> dict:
    key = jax.random.key(seed)
    ks = jax.random.split(key, 24)
    f32 = jnp.float32
    D, d_a, d_n = D_MODEL, DIFF_HEAD_DIM, NSA_HEAD_DIM
    nrm = lambda k, shape, s: jax.random.normal(k, shape, f32) * s
    gain = lambda k, shape: 1.0 + 0.02 * jax.random.normal(k, shape, f32)
    return {
        "x": nrm(ks[0], (BATCH, SEQ, D), 1.0),
        "c": nrm(ks[1], (BATCH, D), 1.0),
        "rel_bias": nrm(ks[2], (REL_BUCKETS, REL_HEADS), 0.5),
        "ada_w": nrm(ks[3], (DEPTH, D, 6 * D), D ** -0.5),
        "ada_b": nrm(ks[4], (DEPTH, 6 * D), 0.02),
        "attn_norm": gain(ks[5], (DEPTH, D)),
        "mlp_norm": gain(ks[6], (DEPTH, D)),
        "mlp_w1": nrm(ks[7], (DEPTH, D, D_FF), D ** -0.5),
        "mlp_w2": nrm(ks[8], (DEPTH, D_FF, D), D_FF ** -0.5),
        "a_w_in": nrm(ks[9], (N_A_LAYERS, D, 3 * D), D ** -0.5),
        "a_w_out": nrm(ks[10], (N_A_LAYERS, D, D), D ** -0.5),
        "a_lambda": nrm(ks[11], (N_A_LAYERS, 4, d_a), 0.1),
        "a_subln": gain(ks[12], (N_A_LAYERS, 2 * d_a)),
        "kv_ada_w": nrm(ks[13], (D, 2 * D), D ** -0.5),
        "kv_ada_b": nrm(ks[14], (2 * D,), 0.02),
        "kv_norm": gain(ks[15], (D,)),
        "w_kv": nrm(ks[16], (D, KV_COLS), D ** -0.5),
        "cmp_pos": nrm(ks[17], (2, CMP_LEN, d_n), 0.1),
        "cmp_w1": nrm(ks[18], (2, CMP_LEN * d_n, CMP_HIDDEN), (CMP_LEN * d_n) ** -0.5),
        "cmp_w2": nrm(ks[19], (2, CMP_HIDDEN, d_n), CMP_HIDDEN ** -0.5),
        "b_w_in": nrm(ks[20], (N_B_LAYERS, D, NSA_IN_COLS), D ** -0.5),
        "b_w_out": nrm(ks[21], (N_B_LAYERS, D, D), D ** -0.5),
        "final_norm": gain(ks[22], (D,)),
    }


def reference(x, c, rel_bias, ada_w, ada_b, attn_norm, mlp_norm, mlp_w1, mlp_w2,
              a_w_in, a_w_out, a_lambda, a_subln, kv_ada_w, kv_ada_b, kv_norm, w_kv,
              cmp_pos, cmp_w1, cmp_w2, b_w_in, b_w_out, final_norm):
    c_act = jax.nn.silu(c)
    shared = None
    for layer in range(DEPTH):
        mod = (c_act @ ada_w[layer] + ada_b[layer])[:, None, :]
        sh_a, sc_a, gt_a, sh_m, sc_m, gt_m = jnp.split(mod, 6, axis=-1)
        h = modulate(rms_norm(x, attn_norm[layer]), sh_a, sc_a)
        if layer < N_A_LAYERS:
            mix = diff_attention(h, a_w_in[layer], a_w_out[layer], a_lambda[layer],
                                 a_subln[layer], rel_bias, layer)
        else:
            i = layer - N_A_LAYERS
            mix = nsa_attention(h, b_w_in[i], b_w_out[i], rel_bias, *shared)
        x = x + gt_a * mix
        h = modulate(rms_norm(x, mlp_norm[layer]), sh_m, sc_m)
        x = x + gt_m * squared_relu_mlp(h, mlp_w1[layer], mlp_w2[layer])
        if layer == N_A_LAYERS - 1:
            kv_mod = (c_act @ kv_ada_w + kv_ada_b)[:, None, :]
            sh_kv, sc_kv = jnp.split(kv_mod, 2, axis=-1)
            h_kv = modulate(rms_norm(x, kv_norm), sh_kv, sc_kv)
            shared = nsa_shared_kv(h_kv, w_kv, cmp_pos, cmp_w1, cmp_w2)
    return rms_norm(x, final_norm)
```

```python
import functools
import math

import numpy as np
import jax
import jax.numpy as jnp
from jax import lax
from jax.experimental import pallas as pl
from jax.experimental.pallas import tpu as pltpu

F32 = jnp.float32
BF16 = jnp.bfloat16

NEG = -1e30
BELOW_NEG = -3e38
NORM_EPS = 1e-6

HEAD_DIM = 64
DIFF_HEADS = 8
NSA_KV_HEADS = 4
NSA_GROUP = 4
REL_BUCKETS = 32
REL_MAX_DIST = 128
CMP_LEN = 32
CMP_STRIDE = 16
SLC_LEN = 64
SLC_TOPK = 16
SLC_FORCED_LOCAL = 2
FORCE_BONUS = 1e4
WINDOW = 512

LANES = 128
VMEM_LIMIT_BYTES = 56 * 1024 * 1024

PROJ_ROWS = 512
DIFF_TQ = 256
DIFF_TK = 256
NSA_TQ = 128
NSA_TK = 256
CMP_TQ = 256
FF_CHUNK = 1024


def _cparams(*sem):
    return pltpu.CompilerParams(dimension_semantics=sem, vmem_limit_bytes=VMEM_LIMIT_BYTES)


def _const_spec(shape):
    nd = len(shape)
    return pl.BlockSpec(shape, lambda *_: (0,) * nd, pipeline_mode=pl.Buffered(1))


def _adaln_kernel(c_ref, w_ref, b_ref, o_ref):
    c = c_ref[...]
    c_act = (c * (1.0 / (1.0 + jnp.exp(-c)))).astype(BF16)
    o_ref[0] = jnp.dot(c_act, w_ref[0].astype(BF16), preferred_element_type=F32) + b_ref[0]


def _adaln(c, w, b, tn=2048):
    L, D, N = w.shape
    B = c.shape[0]
    tn = min(tn, N)
    return pl.pallas_call(
        _adaln_kernel,
        grid=(L, N // tn),
        in_specs=[
            pl.BlockSpec((B, D), lambda l, j: (0, 0)),
            pl.BlockSpec((1, D, tn), lambda l, j: (l, 0, j)),
            pl.BlockSpec((1, 1, tn), lambda l, j: (l, 0, j)),
        ],
        out_specs=pl.BlockSpec((1, B, tn), lambda l, j: (l, 0, j)),
        out_shape=jax.ShapeDtypeStruct((L, B, N), F32),
        compiler_params=_cparams("arbitrary", "arbitrary"),
        name="adaln",
    )(c, w, b.reshape(L, 1, N))


def _rms_mod(x, g, shift, scale):
    ms = jnp.mean(x * x, axis=-1, keepdims=True)
    y = x * lax.rsqrt(ms + NORM_EPS) * g
    return y * (1.0 + scale) + shift


def _proj_kernel(x_ref, g_ref, sh_ref, sc_ref, *refs, n_w, plan, rows):
    w_refs, o_refs = refs[:n_w], refs[n_w:]
    h = _rms_mod(x_ref[0], g_ref[...], sh_ref[0], sc_ref[0]).astype(BF16)
    for (wi, c0, width, oi, o0, mode) in plan:
        y = jnp.dot(h, w_refs[wi][:, c0:c0 + width], preferred_element_type=F32)
        if mode == "scale":
            y = y * (HEAD_DIM ** -0.5)
        elif mode == "sigmoid":
            y = 1.0 / (1.0 + jnp.exp(-y))
        elif mode == "onehot":
            pos = pl.program_id(1) * rows + lax.broadcasted_iota(jnp.int32, y.shape, 0)
            lane = lax.broadcasted_iota(jnp.int32, y.shape, 1) % LANES
            hot = (lane - HEAD_DIM) == lax.shift_right_logical(pos, int(math.log2(SLC_LEN)))
            y = y + jnp.where(hot, 1.0, 0.0)
        o_refs[oi][0, :, o0:o0 + width] = y.astype(o_refs[oi].dtype)


def _proj(x, g, shift, scale, weights, outs, plan, name):
    B, T, D = x.shape
    rows = min(PROJ_ROWS, T)
    in_specs = [
        pl.BlockSpec((1, rows, D), lambda b, i: (b, i, 0)),
        _const_spec((1, D)),
        pl.BlockSpec((1, 1, D), lambda b, i: (b, 0, 0)),
        pl.BlockSpec((1, 1, D), lambda b, i: (b, 0, 0)),
    ] + [_const_spec(w.shape) for w in weights]
    out_specs = [pl.BlockSpec((1, rows, n), lambda b, i: (b, i, 0)) for n, _ in outs]
    out_shape = [jax.ShapeDtypeStruct((B, T, n), dt) for n, dt in outs]
    res = pl.pallas_call(
        functools.partial(_proj_kernel, n_w=len(weights), plan=tuple(plan), rows=rows),
        grid=(B, T // rows),
        in_specs=in_specs,
        out_specs=out_specs,
        out_shape=out_shape,
        compiler_params=_cparams("arbitrary", "arbitrary"),
        name=name,
    )(x, g.reshape(1, D), shift.reshape(B, 1, D), scale.reshape(B, 1, D), *weights)
    return res


def _chunks(n, step=512):
    return [(c, min(step, n - c)) for c in range(0, n, step)]


def _outproj_kernel(x_ref, o_ref, w_ref, gt_ref, y_ref):
    y = jnp.dot(o_ref[0], w_ref[...], preferred_element_type=F32)
    y_ref[0] = x_ref[0] + gt_ref[0] * y


def _outproj_residual(x, o, w, gate):
    B, T, D = x.shape
    rows = min(PROJ_ROWS, T)
    return pl.pallas_call(
        _outproj_kernel,
        grid=(B, T // rows),
        in_specs=[
            pl.BlockSpec((1, rows, D), lambda b, i: (b, i, 0)),
            pl.BlockSpec((1, rows, D), lambda b, i: (b, i, 0)),
            _const_spec(w.shape),
            pl.BlockSpec((1, 1, D), lambda b, i: (b, 0, 0)),
        ],
        out_specs=pl.BlockSpec((1, rows, D), lambda b, i: (b, i, 0)),
        out_shape=jax.ShapeDtypeStruct((B, T, D), F32),
        compiler_params=_cparams("arbitrary", "arbitrary"),
        name="outproj_residual",
    )(x, o, w, gate.reshape(B, 1, D))


def _mlp_kernel(x_ref, g_ref, sh_ref, sc_ref, gt_ref, w1_ref, w2_ref, fg_ref, y_ref, *, d_ff, final):
    x = x_ref[0]
    h = _rms_mod(x, g_ref[...], sh_ref[0], sc_ref[0]).astype(BF16)
    acc = jnp.zeros(x.shape, F32)
    for c0 in range(0, d_ff, FF_CHUNK):
        a = jnp.dot(h, w1_ref[:, c0:c0 + FF_CHUNK], preferred_element_type=F32)
        a = jnp.square(jnp.maximum(a, 0.0)).astype(BF16)
        acc = acc + jnp.dot(a, w2_ref[c0:c0 + FF_CHUNK, :], preferred_element_type=F32)
    y = x + gt_ref[0] * acc
    if final:
        ms = jnp.mean(y * y, axis=-1, keepdims=True)
        y = y * lax.rsqrt(ms + NORM_EPS) * fg_ref[...]
    y_ref[0] = y


def _mlp_residual(x, g, shift, scale, gate, w1, w2, final_gain, final):
    B, T, D = x.shape
    d_ff = w1.shape[1]
    rows = min(PROJ_ROWS, T)
    vec = pl.BlockSpec((1, 1, D), lambda b, i: (b, 0, 0))
    return pl.pallas_call(
        functools.partial(_mlp_kernel, d_ff=d_ff, final=final),
        grid=(B, T // rows),
        in_specs=[
            pl.BlockSpec((1, rows, D), lambda b, i: (b, i, 0)),
            _const_spec((1, D)), vec, vec, vec,
            _const_spec(w1.shape), _const_spec(w2.shape),
            _const_spec((1, D)),
        ],
        out_specs=pl.BlockSpec((1, rows, D), lambda b, i: (b, i, 0)),
        out_shape=jax.ShapeDtypeStruct((B, T, D), F32),
        compiler_params=_cparams("arbitrary", "arbitrary"),
        name="mlp_residual",
    )(x, g.reshape(1, D), shift.reshape(B, 1, D), scale.reshape(B, 1, D), gate.reshape(B, 1, D),
      w1, w2, final_gain.reshape(1, D))


def _t5_bucket(dist):
    n = jnp.maximum(dist, 0)
    max_exact = REL_BUCKETS // 2
    nf = jnp.maximum(n, 1).astype(F32)
    large = max_exact + (jnp.log(nf / max_exact) / math.log(REL_MAX_DIST / max_exact)
                         * (REL_BUCKETS - max_exact)).astype(jnp.int32)
    large = jnp.minimum(large, REL_BUCKETS - 1)
    return jnp.where(n < max_exact, n, large)


def _bias_tiles(rel_bias, rels, tq, tk, window):
    iq = np.arange(tq)[:, None]
    ik = np.arange(tk)[None, :]
    dist = jnp.asarray(np.stack([r + iq - ik for r in rels]).astype(np.int32))
    tab = rel_bias.astype(F32).T
    b = tab[:, _t5_bucket(dist)]
    ok = dist >= 0
    if window is not None:
        ok = ok & (dist < window)
    return jnp.where(ok[None], b, NEG)


def _softmax_tile(q, k, v, bias, m_sc, l_sc, acc_sc):
    s = lax.dot_general(q, k, (((1,), (1,)), ((), ())), preferred_element_type=F32) + bias
    m_prev = m_sc[...]
    m_new = jnp.maximum(m_prev, jnp.max(s, axis=-1, keepdims=True))
    alpha = jnp.exp(m_prev - m_new)
    p = jnp.exp(s - m_new)
    l_sc[...] = alpha * l_sc[...] + jnp.sum(p, axis=-1, keepdims=True)
    acc_sc[...] = alpha * acc_sc[...] + jnp.dot(p.astype(BF16), v, preferred_element_type=F32)
    m_sc[...] = m_new


def _init_stats(m_sc, l_sc, acc_sc):
    m_sc[...] = jnp.full(m_sc.shape, NEG, F32)
    l_sc[...] = jnp.zeros(l_sc.shape, F32)
    acc_sc[...] = jnp.zeros(acc_sc.shape, F32)


def _diff_attn_kernel(q_ref, k_ref, v_ref, bias_ref, cfar_ref, lam_ref, subln_ref, o_ref,
                      m_sc, l_sc, acc_sc, *, tq, tk, lam_init):
    qi = pl.program_id(2)
    q = q_ref[0]
    lane = lax.broadcasted_iota(jnp.int32, q.shape, 1)
    zero = jnp.zeros_like(q)
    q2 = jnp.concatenate([jnp.where(lane < HEAD_DIM, q, zero),
                          jnp.where(lane >= HEAD_DIM, q, zero)], axis=0)

    def tile(j, bias):
        k = k_ref[0, pl.ds(j * tk, tk), :]
        v = v_ref[0, pl.ds(j * tk, tk), :]
        _softmax_tile(q2, k, v, bias, m_sc, l_sc, acc_sc)

    _init_stats(m_sc, l_sc, acc_sc)
    tile(qi, bias_ref[0, 0])

    @pl.when(qi >= 1)
    def _():
        tile(qi - 1, bias_ref[0, 1])

    cfar = cfar_ref[0]

    def far(j, carry):
        tile(j, cfar)
        return carry

    lax.fori_loop(0, jnp.maximum(qi - 1, 0), far, 0)

    lf = lam_ref[...]
    lam = (jnp.exp(jnp.sum(lf[0:1] * lf[1:2], axis=-1, keepdims=True))
           - jnp.exp(jnp.sum(lf[2:3] * lf[3:4], axis=-1, keepdims=True)) + lam_init)
    on = acc_sc[...] / l_sc[...]
    o = on[:tq] - lam * on[tq:]
    ms = jnp.mean(o * o, axis=-1, keepdims=True)
    o = o * lax.rsqrt(ms + NORM_EPS) * subln_ref[...] * (1.0 - lam_init)
    o_ref[0] = o.astype(o_ref.dtype)


def _diff_attention(qkv, rel_bias, lam, subln, layer_idx):
    B, T, D3 = qkv.shape
    D = D3 // 3
    H = DIFF_HEADS
    tq = min(DIFF_TQ, T)
    tk = tq
    lam_init = 0.8 - 0.6 * math.exp(-0.3 * layer_idx)
    bt = _bias_tiles(rel_bias, [0, tq], tq, tk, None)
    bias = bt.reshape(H, 2, 2, tq, tk).transpose(0, 2, 1, 3, 4).reshape(H, 2, 2 * tq, tk)
    cfar = jnp.repeat(rel_bias.astype(F32)[REL_BUCKETS - 1].reshape(H, 2), tq, axis=1).reshape(H, 2 * tq, 1)
    hb = D // LANES
    return pl.pallas_call(
        functools.partial(_diff_attn_kernel, tq=tq, tk=tk, lam_init=lam_init),
        grid=(B, H, T // tq),
        in_specs=[
            pl.BlockSpec((1, tq, LANES), lambda b, h, i: (b, i, h)),
            pl.BlockSpec((1, T, LANES), lambda b, h, i: (b, 0, hb + h)),
            pl.BlockSpec((1, T, LANES), lambda b, h, i: (b, 0, 2 * hb + h)),
            pl.BlockSpec((1, 2, 2 * tq, tk), lambda b, h, i: (h, 0, 0, 0)),
            pl.BlockSpec((1, 2 * tq, 1), lambda b, h, i: (h, 0, 0)),
            _const_spec(lam.shape),
            _const_spec((1, LANES)),
        ],
        out_specs=pl.BlockSpec((1, tq, LANES), lambda b, h, i: (b, i, h)),
        out_shape=jax.ShapeDtypeStruct((B, T, D), BF16),
        scratch_shapes=[pltpu.VMEM((2 * tq, 1), F32), pltpu.VMEM((2 * tq, 1), F32),
                        pltpu.VMEM((2 * tq, LANES), F32)],
        compiler_params=_cparams("arbitrary", "arbitrary", "arbitrary"),
        name="diff_attention",
    )(qkv, qkv, qkv, bias, cfar, lam.astype(F32), subln.astype(F32).reshape(1, LANES))


def _compress_kernel(a_ref, w1_ref, pos_ref, w2_ref, o_ref):
    outs = []
    half = CMP_STRIDE * HEAD_DIM
    for s in range(2):
        a = a_ref[s, 0, 0]
        w1 = w1_ref[s]
        p = jnp.dot(a, w1[:half], preferred_element_type=F32)
        q = jnp.dot(a, w1[half:], preferred_element_type=F32)
        posb = jnp.dot(jnp.broadcast_to(pos_ref[s], (8, 2 * half)).astype(BF16), w1,
                       preferred_element_type=F32)[0:1]
        hid = p + pltpu.roll(q, q.shape[0] - 1, axis=0) + posb
        hid = 0.5 * hid * (1.0 + jnp.tanh(math.sqrt(2.0 / math.pi) * (hid + 0.044715 * hid * hid * hid)))
        outs.append(jnp.dot(hid.astype(BF16), w2_ref[s], preferred_element_type=F32))
    o_ref[0, 0] = jnp.concatenate(outs, axis=1).astype(o_ref.dtype)


def _compress(kv_cmp, cmp_pos, cmp_w1, cmp_w2):
    B, T, _ = kv_cmp.shape
    G, d = NSA_KV_HEADS, HEAD_DIM
    n = T // CMP_STRIDE
    a = kv_cmp.reshape(B, n, CMP_STRIDE, 2, G, d).transpose(3, 0, 4, 1, 2, 5).reshape(2, B, G, n, CMP_STRIDE * d)
    hid = cmp_w1.shape[-1]
    return pl.pallas_call(
        _compress_kernel,
        grid=(B, G),
        in_specs=[
            pl.BlockSpec((2, 1, 1, n, CMP_STRIDE * d), lambda b, g: (0, b, g, 0, 0)),
            _const_spec((2, CMP_LEN * d, hid)),
            _const_spec((2, 1, CMP_LEN * d)),
            _const_spec((2, hid, d)),
        ],
        out_specs=pl.BlockSpec((1, 1, n, 2 * d), lambda b, g: (b, g, 0, 0)),
        out_shape=jax.ShapeDtypeStruct((B, G, n, 2 * d), BF16),
        compiler_params=_cparams("arbitrary", "arbitrary"),
        name="nsa_compress",
    )(a, cmp_w1.astype(BF16), cmp_pos.astype(F32).reshape(2, 1, CMP_LEN * d), cmp_w2.astype(BF16))


def _stack_heads(qb, tail):
    parts = [jnp.concatenate([qb[:, r * HEAD_DIM:(r + 1) * HEAD_DIM], tail], axis=1) for r in range(NSA_GROUP)]
    return jnp.concatenate(parts, axis=0)


def _nsa_cmp_kernel(q_ref, kc_ref, ov_ref, tri_ref, oc_ref, sel_ref, *, tq, top_k):
    qi = pl.program_id(2)
    R = NSA_GROUP
    n_cmp = kc_ref.shape[2]
    kc = kc_ref[0, 0]
    q4 = _stack_heads(q_ref[0], jnp.zeros((tq, HEAD_DIM), BF16))
    s = lax.dot_general(q4, kc, (((1,), (1,)), ((), ())), preferred_element_type=F32)
    qpos = qi * tq + lax.broadcasted_iota(jnp.int32, (tq, 1), 0)
    cmp_end = lax.broadcasted_iota(jnp.int32, (1, n_cmp), 1) * CMP_STRIDE + (CMP_LEN - 1)
    valid = cmp_end <= qpos
    valid4 = jnp.concatenate([valid] * R, axis=0)
    sm = jnp.where(valid4, s, NEG)
    e = jnp.exp(sm - jnp.max(sm, axis=-1, keepdims=True))
    p = jnp.where(valid4, e / jnp.sum(e, axis=-1, keepdims=True), 0.0)
    oc = jnp.dot(p.astype(BF16), kc, preferred_element_type=F32)
    oc_ref[0] = jnp.concatenate([oc[r * tq:(r + 1) * tq, HEAD_DIM:] for r in range(R)], axis=1)

    psum = p[0:tq]
    for r in range(1, R):
        psum = psum + p[r * tq:(r + 1) * tq]
    ov = ov_ref[...]
    hi = psum.astype(BF16)
    r1 = psum - hi.astype(F32)
    mid = r1.astype(BF16)
    lo = (r1 - mid.astype(F32)).astype(BF16)
    imp = (jnp.dot(hi, ov, preferred_element_type=F32) + jnp.dot(mid, ov, preferred_element_type=F32)
           + jnp.dot(lo, ov, preferred_element_type=F32))
    j = lax.broadcasted_iota(jnp.int32, (1, LANES), 1)
    qblk = lax.shift_right_logical(qpos, int(math.log2(SLC_LEN)))
    forced = (j == 0) | ((j <= qblk) & (j > qblk - SLC_FORCED_LOCAL))
    imp = jnp.where(forced, FORCE_BONUS, imp)
    causal = j <= qblk
    imp = jnp.where(causal, imp, NEG)

    n_slc = tri_ref.shape[0]
    vals = imp.T[:n_slc]
    rem = vals
    cnt = jnp.zeros((1, tq), F32)
    thr = jnp.full((1, tq), NEG, F32)
    for _ in range(top_k):
        mx = jnp.max(rem, axis=0, keepdims=True)
        active = cnt < top_k
        thr = jnp.where(active, mx, thr)
        hit = rem == mx
        cnt = cnt + jnp.where(active, jnp.sum(jnp.where(hit, 1.0, 0.0), axis=0, keepdims=True), 0.0)
        rem = jnp.where(hit, BELOW_NEG, rem)
    above = vals > thr
    n_above = jnp.sum(jnp.where(above, 1.0, 0.0), axis=0, keepdims=True)
    tie = vals == thr
    prefix = jnp.dot(tri_ref[...], jnp.where(tie, 1.0, 0.0).astype(BF16), preferred_element_type=F32)
    chosen = (above | (tie & (prefix <= top_k - n_above))) & (vals > NEG)
    selb = jnp.where(chosen, 0.0, NEG)
    pad_rows = 2 * HEAD_DIM - n_slc - HEAD_DIM
    pieces = [jnp.zeros((HEAD_DIM, tq), F32), selb]
    if pad_rows:
        pieces.append(jnp.zeros((pad_rows, tq), F32))
    sel_ref[0, 0] = jnp.concatenate(pieces, axis=0).T.astype(sel_ref.dtype)


def _nsa_compressed_select(q, kc):
    B, T, D = q.shape
    G, R = NSA_KV_HEADS, NSA_GROUP
    n_cmp = kc.shape[2]
    n_slc = T // SLC_LEN
    assert n_slc <= HEAD_DIM, "selection mask is carried in the 64 spare contraction lanes"
    top_k = min(SLC_TOPK, n_slc)
    tq = min(CMP_TQ, T)
    nn = np.arange(n_cmp)[:, None]
    jj = np.arange(LANES)[None, :]
    overlap = ((nn * CMP_STRIDE < jj * SLC_LEN + SLC_LEN) & (nn * CMP_STRIDE + CMP_LEN - 1 >= jj * SLC_LEN)
               & (jj < n_slc))
    overlap = overlap & (nn < (T - CMP_LEN) // CMP_STRIDE + 1)
    tri = np.tril(np.ones((n_slc, n_slc), np.float32))
    return pl.pallas_call(
        functools.partial(_nsa_cmp_kernel, tq=tq, top_k=top_k),
        grid=(B, G, T // tq),
        in_specs=[
            pl.BlockSpec((1, tq, R * HEAD_DIM), lambda b, g, i: (b, i, g)),
            pl.BlockSpec((1, 1, n_cmp, LANES), lambda b, g, i: (b, g, 0, 0)),
            _const_spec((n_cmp, LANES)),
            _const_spec((n_slc, n_slc)),
        ],
        out_specs=[
            pl.BlockSpec((1, tq, R * HEAD_DIM), lambda b, g, i: (b, i, g)),
            pl.BlockSpec((1, 1, tq, LANES), lambda b, g, i: (b, g, i, 0)),
        ],
        out_shape=[jax.ShapeDtypeStruct((B, T, D), F32), jax.ShapeDtypeStruct((B, G, T, LANES), BF16)],
        compiler_params=_cparams("arbitrary", "arbitrary", "arbitrary"),
        name="nsa_compressed_select",
    )(q, kc, jnp.asarray(overlap.astype(np.float32), BF16), jnp.asarray(tri, BF16))


def _nsa_main_kernel(q_ref, sel_ref, ka_ref, vs_ref, kw_ref, bias_ref, cfar_ref, g_ref, oc_ref, o_ref,
                     ms_sc, ls_sc, as_sc, mw_sc, lw_sc, aw_sc, *, tq, tk, n_win_tiles):
    qi = pl.program_id(2)
    R = NSA_GROUP
    ratio = tk // tq
    par = qi % ratio
    jd = qi // ratio
    q4w = _stack_heads(q_ref[0], jnp.zeros((tq, HEAD_DIM), BF16))
    q4s = q4w + jnp.concatenate([sel_ref[0, 0]] * R, axis=0)

    _init_stats(ms_sc, ls_sc, as_sc)
    _init_stats(mw_sc, lw_sc, aw_sc)

    def tile_s(j, bias):
        _softmax_tile(q4s, ka_ref[0, pl.ds(j * tk, tk), :], vs_ref[0, pl.ds(j * tk, tk), :], bias,
                      ms_sc, ls_sc, as_sc)

    def tile_w(j, bias):
        kv = kw_ref[0, pl.ds(j * tk, tk), :]
        _softmax_tile(q4w, kv, kv, bias, mw_sc, lw_sc, aw_sc)

    for t in range(n_win_tiles):
        if t == 0:
            tile_s(jd, bias_ref[0, par * n_win_tiles])
            tile_w(jd, bias_ref[0, par * n_win_tiles])
        else:
            @pl.when(jd >= t)
            def _():
                b = bias_ref[0, par * n_win_tiles + t]
                if t == 1:
                    tile_s(jd - 1, b)
                tile_w(jd - t, b)

    cfar = cfar_ref[0]

    def far(j, carry):
        tile_s(j, cfar)
        return carry

    lax.fori_loop(0, jnp.maximum(jd - 1, 0), far, 0)

    o_s = as_sc[...] / ls_sc[...]
    o_w = aw_sc[...] / lw_sc[...]
    gts = g_ref[0, 0]
    oc = oc_ref[0]
    outs = []
    for r in range(R):
        rows = slice(r * tq, (r + 1) * tq)
        o_r = (gts[:, 3 * r:3 * r + 1] * oc[:, r * HEAD_DIM:(r + 1) * HEAD_DIM]
               + gts[:, 3 * r + 1:3 * r + 2] * o_s[rows, HEAD_DIM:]
               + gts[:, 3 * r + 2:3 * r + 3] * o_w[rows, HEAD_DIM:])
        outs.append(o_r)
    o_ref[0] = jnp.concatenate(outs, axis=1).astype(o_ref.dtype)


def _nsa_main(q, sel, kv, gates, o_cmp, rel_bias):
    B, T, D = q.shape
    G, R = NSA_KV_HEADS, NSA_GROUP
    tq = min(NSA_TQ, T)
    tk = min(NSA_TK, T)
    ratio = tk // tq
    n_win_tiles = (WINDOW + tk - 1) // tk + 1
    rels = [par * tq + t * tk for par in range(ratio) for t in range(n_win_tiles)]
    assert (ratio - 1) * tq + tk + tq - 1 < WINDOW and 2 * tk - (tk - 1) >= REL_MAX_DIST
    bt = _bias_tiles(rel_bias, rels, tq, tk, WINDOW)
    n = len(rels)
    bias = bt.reshape(G, R, n, tq, tk).transpose(0, 2, 1, 3, 4).reshape(G, n, R * tq, tk)
    cfar = jnp.repeat(rel_bias.astype(F32)[REL_BUCKETS - 1].reshape(G, R), tq, axis=1).reshape(G, R * tq, 1)
    return pl.pallas_call(
        functools.partial(_nsa_main_kernel, tq=tq, tk=tk, n_win_tiles=n_win_tiles),
        grid=(B, G, T // tq),
        in_specs=[
            pl.BlockSpec((1, tq, R * HEAD_DIM), lambda b, g, i: (b, i, g)),
            pl.BlockSpec((1, 1, tq, LANES), lambda b, g, i: (b, g, i, 0)),
            pl.BlockSpec((1, T, LANES), lambda b, g, i: (b, 0, 3 * g)),
            pl.BlockSpec((1, T, LANES), lambda b, g, i: (b, 0, 3 * g + 1)),
            pl.BlockSpec((1, T, LANES), lambda b, g, i: (b, 0, 3 * g + 2)),
            pl.BlockSpec((1, n, R * tq, tk), lambda b, g, i: (g, 0, 0, 0)),
            pl.BlockSpec((1, R * tq, 1), lambda b, g, i: (g, 0, 0)),
            pl.BlockSpec((1, 1, tq, 3 * R), lambda b, g, i: (b, g, i, 0)),
            pl.BlockSpec((1, tq, R * HEAD_DIM), lambda b, g, i: (b, i, g)),
        ],
        out_specs=pl.BlockSpec((1, tq, R * HEAD_DIM), lambda b, g, i: (b, i, g)),
        out_shape=jax.ShapeDtypeStruct((B, T, D), BF16),
        scratch_shapes=[pltpu.VMEM((R * tq, 1), F32), pltpu.VMEM((R * tq, 1), F32), pltpu.VMEM((R * tq, LANES), F32),
                        pltpu.VMEM((R * tq, 1), F32), pltpu.VMEM((R * tq, 1), F32), pltpu.VMEM((R * tq, LANES), F32)],
        compiler_params=_cparams("arbitrary", "arbitrary", "arbitrary"),
        name="nsa_main",
    )(q, sel, kv, kv, kv, bias, cfar, gates, o_cmp)


def kernel(x, c, rel_bias, ada_w, ada_b, attn_norm, mlp_norm, mlp_w1, mlp_w2, a_w_in, a_w_out, a_lambda, a_subln, kv_ada_w, kv_ada_b, kv_norm, w_kv, cmp_pos, cmp_w1, cmp_w2, b_w_in, b_w_out, final_norm):
    B, T, D = x.shape
    depth = ada_w.shape[0]
    n_a = a_w_in.shape[0]
    G, R, d = NSA_KV_HEADS, NSA_GROUP, HEAD_DIM

    mod = _adaln(c, ada_w, ada_b)
    kv_mod = _adaln(c, kv_ada_w[None], kv_ada_b[None])[0]

    wkv = w_kv.reshape(D, 6, G, d)
    z = jnp.zeros((D, G, d), w_kv.dtype)
    w_cmp = wkv[:, 0:2].reshape(D, 2 * G * d).astype(BF16)
    w_sw = jnp.stack([wkv[:, 2], z, z, wkv[:, 3], wkv[:, 4], wkv[:, 5]], axis=2).reshape(D, G * 6 * d).astype(BF16)
    kv_plan = [(0, c0, w, 0, c0, "plain") for c0, w in _chunks(2 * G * d)]
    kv_plan += [(1, c0, w, 1, c0, "onehot" if (c0 // LANES) % 3 == 0 else "plain") for c0, w in _chunks(G * 6 * d, LANES)]

    shared = None
    for layer in range(depth):
        sh_a, sc_a, gt_a, sh_m, sc_m, gt_m = jnp.split(mod[layer], 6, axis=-1)
        if layer < n_a:
            w_in = a_w_in[layer].astype(BF16)
            plan = [(0, c0, w, 0, c0, "scale" if c0 < D else "plain") for c0, w in _chunks(3 * D)]
            (qkv,) = _proj(x, attn_norm[layer], sh_a, sc_a, [w_in], [(3 * D, BF16)], plan, "diff_qkv_proj")
            mix = _diff_attention(qkv, rel_bias, a_lambda[layer], a_subln[layer], layer)
            w_out = a_w_out[layer].astype(BF16)
        else:
            i = layer - n_a
            w_q = b_w_in[i][:, :D].astype(BF16)
            w_g = b_w_in[i][:, D:].astype(BF16)
            plan = [(0, c0, w, 0, c0, "scale") for c0, w in _chunks(D)] + [(1, 0, 3 * G * R, 1, 0, "sigmoid")]
            q, gates = _proj(x, attn_norm[layer], sh_a, sc_a, [w_q, w_g], [(D, BF16), (3 * G * R, F32)], plan,
                             "nsa_in_proj")
            gates = gates.reshape(B, T, G, 3 * R).transpose(0, 2, 1, 3)
            kc, kv_sw = shared
            o_cmp, sel = _nsa_compressed_select(q, kc)
            mix = _nsa_main(q, sel, kv_sw, gates, o_cmp, rel_bias)
            w_out = b_w_out[i].astype(BF16)
        x = _outproj_residual(x, mix, w_out, gt_a)
        x = _mlp_residual(x, mlp_norm[layer], sh_m, sc_m, gt_m, mlp_w1[layer].astype(BF16),
                          mlp_w2[layer].astype(BF16), final_norm, final=(layer == depth - 1))
        if layer == n_a - 1:
            sh_kv, sc_kv = jnp.split(kv_mod, 2, axis=-1)
            kv_cmp, kv_sw = _proj(x, kv_norm, sh_kv, sc_kv, [w_cmp, w_sw], [(2 * G * d, BF16), (G * 6 * d, BF16)],
                                  kv_plan, "nsa_kv_proj")
            shared = (_compress(kv_cmp, cmp_pos, cmp_w1, cmp_w2), kv_sw)
    return x
```

```python
import functools
import math

import numpy as np
import jax
import jax.numpy as jnp
from jax import lax
from jax.experimental import pallas as pl
from jax.experimental.pallas import tpu as pltpu

F32 = jnp.float32
BF16 = jnp.bfloat16

NEG = -1e30
BELOW_NEG = -3e38
NORM_EPS = 1e-6

HEAD_DIM = 64
DIFF_HEADS = 8
NSA_KV_HEADS = 4
NSA_GROUP = 4
REL_BUCKETS = 32
REL_MAX_DIST = 128
CMP_LEN = 32
CMP_STRIDE = 16
SLC_LEN = 64
SLC_TOPK = 16
SLC_FORCED_LOCAL = 2
FORCE_BONUS = 1e4
WINDOW = 512

LANES = 128
VMEM_LIMIT_BYTES = 56 * 1024 * 1024

PROJ_ROWS = 512
DIFF_TQ = 256
DIFF_TK = 256
FAR_UNROLL_LOG2 = 2
FAR_UNROLL = 1 << FAR_UNROLL_LOG2
LOG2E = math.log2(math.e)
NSA_TQ = 128
NSA_TK = 256
CMP_TQ = 256
FF_CHUNK = 1024


def _cparams(*sem):
    return pltpu.CompilerParams(dimension_semantics=sem, vmem_limit_bytes=VMEM_LIMIT_BYTES)


def _const_spec(shape):
    nd = len(shape)
    return pl.BlockSpec(shape, lambda *_: (0,) * nd, pipeline_mode=pl.Buffered(1))


def _adaln_kernel(c_ref, w_ref, b_ref, o_ref):
    c = c_ref[...]
    c_act = (c * (1.0 / (1.0 + jnp.exp(-c)))).astype(BF16)
    o_ref[0] = jnp.dot(c_act, w_ref[0].astype(BF16), preferred_element_type=F32) + b_ref[0]


def _adaln(c, w, b, tn=2048):
    L, D, N = w.shape
    B = c.shape[0]
    tn = min(tn, N)
    return pl.pallas_call(
        _adaln_kernel,
        grid=(L, N // tn),
        in_specs=[
            pl.BlockSpec((B, D), lambda l, j: (0, 0)),
            pl.BlockSpec((1, D, tn), lambda l, j: (l, 0, j)),
            pl.BlockSpec((1, 1, tn), lambda l, j: (l, 0, j)),
        ],
        out_specs=pl.BlockSpec((1, B, tn), lambda l, j: (l, 0, j)),
        out_shape=jax.ShapeDtypeStruct((L, B, N), F32),
        compiler_params=_cparams("arbitrary", "arbitrary"),
        name="adaln",
    )(c, w, b.reshape(L, 1, N))


def _rms_mod(x, g, shift, scale):
    ms = jnp.mean(x * x, axis=-1, keepdims=True)
    y = x * lax.rsqrt(ms + NORM_EPS) * g
    return y * (1.0 + scale) + shift


def _proj_kernel(x_ref, g_ref, sh_ref, sc_ref, *refs, n_w, plan, rows):
    w_refs, o_refs = refs[:n_w], refs[n_w:]
    h = _rms_mod(x_ref[0], g_ref[...], sh_ref[0], sc_ref[0]).astype(BF16)
    for (wi, c0, width, oi, o0, mode) in plan:
        y = jnp.dot(h, w_refs[wi][:, c0:c0 + width], preferred_element_type=F32)
        if mode == "scale":
            y = y * (HEAD_DIM ** -0.5 * LOG2E)
        elif mode == "sigmoid":
            y = 1.0 / (1.0 + jnp.exp(-y))
        elif mode == "onehot":
            pos = pl.program_id(1) * rows + lax.broadcasted_iota(jnp.int32, y.shape, 0)
            lane = lax.broadcasted_iota(jnp.int32, y.shape, 1) % LANES
            hot = (lane - HEAD_DIM) == lax.shift_right_logical(pos, int(math.log2(SLC_LEN)))
            y = y + jnp.where(hot, 1.0, 0.0)
        o_refs[oi][0, :, o0:o0 + width] = y.astype(o_refs[oi].dtype)


def _proj(x, g, shift, scale, weights, outs, plan, name):
    B, T, D = x.shape
    rows = min(PROJ_ROWS, T)
    in_specs = [
        pl.BlockSpec((1, rows, D), lambda b, i: (b, i, 0)),
        _const_spec((1, D)),
        pl.BlockSpec((1, 1, D), lambda b, i: (b, 0, 0)),
        pl.BlockSpec((1, 1, D), lambda b, i: (b, 0, 0)),
    ] + [_const_spec(w.shape) for w in weights]
    out_specs = [pl.BlockSpec((1, rows, n), lambda b, i: (b, i, 0)) for n, _ in outs]
    out_shape = [jax.ShapeDtypeStruct((B, T, n), dt) for n, dt in outs]
    res = pl.pallas_call(
        functools.partial(_proj_kernel, n_w=len(weights), plan=tuple(plan), rows=rows),
        grid=(B, T // rows),
        in_specs=in_specs,
        out_specs=out_specs,
        out_shape=out_shape,
        compiler_params=_cparams("arbitrary", "arbitrary"),
        name=name,
    )(x, g.reshape(1, D), shift.reshape(B, 1, D), scale.reshape(B, 1, D), *weights)
    return res


def _chunks(n, step=512):
    return [(c, min(step, n - c)) for c in range(0, n, step)]


def _outproj_kernel(x_ref, o_ref, w_ref, gt_ref, y_ref):
    y = jnp.dot(o_ref[0], w_ref[...], preferred_element_type=F32)
    y_ref[0] = x_ref[0] + gt_ref[0] * y


def _outproj_residual(x, o, w, gate):
    B, T, D = x.shape
    rows = min(PROJ_ROWS, T)
    return pl.pallas_call(
        _outproj_kernel,
        grid=(B, T // rows),
        in_specs=[
            pl.BlockSpec((1, rows, D), lambda b, i: (b, i, 0)),
            pl.BlockSpec((1, rows, D), lambda b, i: (b, i, 0)),
            _const_spec(w.shape),
            pl.BlockSpec((1, 1, D), lambda b, i: (b, 0, 0)),
        ],
        out_specs=pl.BlockSpec((1, rows, D), lambda b, i: (b, i, 0)),
        out_shape=jax.ShapeDtypeStruct((B, T, D), F32),
        compiler_params=_cparams("arbitrary", "arbitrary"),
        name="outproj_residual",
    )(x, o, w, gate.reshape(B, 1, D))


def _mlp_kernel(x_ref, g_ref, sh_ref, sc_ref, gt_ref, w1_ref, w2_ref, fg_ref, y_ref, *, d_ff, final):
    x = x_ref[0]
    h = _rms_mod(x, g_ref[...], sh_ref[0], sc_ref[0]).astype(BF16)
    acc = jnp.zeros(x.shape, F32)
    for c0 in range(0, d_ff, FF_CHUNK):
        a = jnp.dot(h, w1_ref[:, c0:c0 + FF_CHUNK], preferred_element_type=F32)
        a = jnp.square(jnp.maximum(a, 0.0)).astype(BF16)
        acc = acc + jnp.dot(a, w2_ref[c0:c0 + FF_CHUNK, :], preferred_element_type=F32)
    y = x + gt_ref[0] * acc
    if final:
        ms = jnp.mean(y * y, axis=-1, keepdims=True)
        y = y * lax.rsqrt(ms + NORM_EPS) * fg_ref[...]
    y_ref[0] = y


def _mlp_residual(x, g, shift, scale, gate, w1, w2, final_gain, final):
    B, T, D = x.shape
    d_ff = w1.shape[1]
    rows = min(PROJ_ROWS, T)
    vec = pl.BlockSpec((1, 1, D), lambda b, i: (b, 0, 0))
    return pl.pallas_call(
        functools.partial(_mlp_kernel, d_ff=d_ff, final=final),
        grid=(B, T // rows),
        in_specs=[
            pl.BlockSpec((1, rows, D), lambda b, i: (b, i, 0)),
            _const_spec((1, D)), vec, vec, vec,
            _const_spec(w1.shape), _const_spec(w2.shape),
            _const_spec((1, D)),
        ],
        out_specs=pl.BlockSpec((1, rows, D), lambda b, i: (b, i, 0)),
        out_shape=jax.ShapeDtypeStruct((B, T, D), F32),
        compiler_params=_cparams("arbitrary", "arbitrary"),
        name="mlp_residual",
    )(x, g.reshape(1, D), shift.reshape(B, 1, D), scale.reshape(B, 1, D), gate.reshape(B, 1, D),
      w1, w2, final_gain.reshape(1, D))


def _t5_bucket(dist):
    n = jnp.maximum(dist, 0)
    max_exact = REL_BUCKETS // 2
    nf = jnp.maximum(n, 1).astype(F32)
    large = max_exact + (jnp.log(nf / max_exact) / math.log(REL_MAX_DIST / max_exact)
                         * (REL_BUCKETS - max_exact)).astype(jnp.int32)
    large = jnp.minimum(large, REL_BUCKETS - 1)
    return jnp.where(n < max_exact, n, large)


def _bias_tiles(rel_bias, rels, tq, tk, window):
    iq = np.arange(tq)[:, None]
    ik = np.arange(tk)[None, :]
    dist = jnp.asarray(np.stack([r + iq - ik for r in rels]).astype(np.int32))
    tab = rel_bias.astype(F32).T
    b = tab[:, _t5_bucket(dist)]
    ok = dist >= 0
    if window is not None:
        ok = ok & (dist < window)
    return jnp.where(ok[None], b * LOG2E, NEG)


def _attend(q, tiles, stats):
    m, l, acc = stats
    for k, v, bias in tiles:
        s = lax.dot_general(q, k, (((1,), (1,)), ((), ())), preferred_element_type=F32)
        parts = [s[:, i:i + LANES] for i in range(0, s.shape[1], LANES)]
        if bias is not None:
            cols = [bias[:, i:i + LANES] for i in range(0, bias.shape[1], LANES)]
            parts = [t + cols[i % len(cols)] for i, t in enumerate(parts)]
        mx = functools.reduce(jnp.maximum, parts)
        m_new = jnp.maximum(m, jnp.max(mx, axis=-1, keepdims=True))
        alpha = jnp.exp2(m - m_new)
        p = [jnp.exp2(t - m_new) for t in parts]
        l = alpha * l + functools.reduce(jnp.add, p)
        pv = jnp.dot(jnp.concatenate(p, axis=1).astype(BF16), v, preferred_element_type=F32)
        acc = alpha * acc + pv
        m = m_new
    return m, l, acc


def _fresh_stats(rows):
    return (jnp.full((rows, LANES), NEG, F32), jnp.zeros((rows, LANES), F32), jnp.zeros((rows, LANES), F32))


def _causal_sweep(q, near, n_far, load_tile, cfar, scratch, side_work=None):
    m_sc, l_sc, acc_sc = scratch
    rows = q.shape[0]
    n_groups = lax.shift_right_logical(n_far, FAR_UNROLL_LOG2)
    rem = n_far - n_groups * FAR_UNROLL

    for r in range(FAR_UNROLL):
        @pl.when(rem == r)
        def _():
            extra = [load_tile(n_groups * FAR_UNROLL + i) + (cfar,) for i in range(r)]
            m, l, acc = _attend(q, near + extra, _fresh_stats(rows))
            m_sc[...], l_sc[...], acc_sc[...] = m - cfar, l, acc
            if side_work is not None:
                side_work()

    def group(g, carry):
        tiles = [load_tile(g * FAR_UNROLL + u) + (None,) for u in range(FAR_UNROLL)]
        m_sc[...], l_sc[...], acc_sc[...] = _attend(q, tiles, (m_sc[...], l_sc[...], acc_sc[...]))
        return carry

    lax.fori_loop(0, n_groups, group, 0)


def _diff_attn_kernel(q_ref, k_ref, v_ref, bias_ref, cfar_ref, lam_ref, subln_ref, o_ref,
                      m_sc, l_sc, acc_sc, *, tq, tk, lam_init):
    qi = pl.program_id(2)
    q = q_ref[0]
    lane = lax.broadcasted_iota(jnp.int32, q.shape, 1)
    zero = jnp.zeros_like(q)
    q2 = jnp.concatenate([jnp.where(lane < HEAD_DIM, q, zero),
                          jnp.where(lane >= HEAD_DIM, q, zero)], axis=0)

    def load_tile(j):
        return (k_ref[0, pl.ds(j * tk, tk), :], v_ref[0, pl.ds(j * tk, tk), :])

    sub_bias = bias_ref[0, 1] + jnp.where(qi >= 1, 0.0, NEG)
    near = [load_tile(qi) + (bias_ref[0, 0],), load_tile(jnp.maximum(qi - 1, 0)) + (sub_bias,)]
    _causal_sweep(q2, near, jnp.maximum(qi - 1, 0), load_tile, cfar_ref[0], (m_sc, l_sc, acc_sc))

    lf = lam_ref[...]
    lam = (jnp.exp(jnp.sum(lf[0:1] * lf[1:2], axis=-1, keepdims=True))
           - jnp.exp(jnp.sum(lf[2:3] * lf[3:4], axis=-1, keepdims=True)) + lam_init)
    on = acc_sc[...] / jnp.sum(l_sc[...], axis=-1, keepdims=True)
    o = on[:tq] - lam * on[tq:]
    ms = jnp.mean(o * o, axis=-1, keepdims=True)
    o = o * lax.rsqrt(ms + NORM_EPS) * subln_ref[...] * (1.0 - lam_init)
    o_ref[0] = o.astype(o_ref.dtype)


def _diff_attention(qkv, rel_bias, lam, subln, layer_idx):
    B, T, D3 = qkv.shape
    D = D3 // 3
    H = DIFF_HEADS
    tq = min(DIFF_TQ, T)
    tk = tq
    lam_init = 0.8 - 0.6 * math.exp(-0.3 * layer_idx)
    bt = _bias_tiles(rel_bias, [0, tq], tq, tk, None)
    bias = bt.reshape(H, 2, 2, tq, tk).transpose(0, 2, 1, 3, 4).reshape(H, 2, 2 * tq, tk)
    cfar = jnp.repeat(rel_bias.astype(F32)[REL_BUCKETS - 1].reshape(H, 2) * LOG2E, tq, axis=1)
    cfar = jnp.broadcast_to(cfar.reshape(H, 2 * tq, 1), (H, 2 * tq, LANES))
    hb = D // LANES
    return pl.pallas_call(
        functools.partial(_diff_attn_kernel, tq=tq, tk=tk, lam_init=lam_init),
        grid=(B, H, T // tq),
        in_specs=[
            pl.BlockSpec((1, tq, LANES), lambda b, h, i: (b, i, h)),
            pl.BlockSpec((1, T, LANES), lambda b, h, i: (b, 0, hb + h)),
            pl.BlockSpec((1, T, LANES), lambda b, h, i: (b, 0, 2 * hb + h)),
            pl.BlockSpec((1, 2, 2 * tq, tk), lambda b, h, i: (h, 0, 0, 0)),
            pl.BlockSpec((1, 2 * tq, LANES), lambda b, h, i: (h, 0, 0)),
            _const_spec(lam.shape),
            _const_spec((1, LANES)),
        ],
        out_specs=pl.BlockSpec((1, tq, LANES), lambda b, h, i: (b, i, h)),
        out_shape=jax.ShapeDtypeStruct((B, T, D), BF16),
        scratch_shapes=[pltpu.VMEM((2 * tq, LANES), F32)] * 3,
        compiler_params=_cparams("arbitrary", "arbitrary", "arbitrary"),
        name="diff_attention",
    )(qkv, qkv, qkv, bias, cfar, lam.astype(F32), subln.astype(F32).reshape(1, LANES))


def _compress_kernel(a_ref, w1_ref, pos_ref, w2_ref, o_ref):
    outs = []
    half = CMP_STRIDE * HEAD_DIM
    for s in range(2):
        a = a_ref[s, 0, 0]
        w1 = w1_ref[s]
        p = jnp.dot(a, w1[:half], preferred_element_type=F32)
        q = jnp.dot(a, w1[half:], preferred_element_type=F32)
        posb = jnp.dot(jnp.broadcast_to(pos_ref[s], (8, 2 * half)).astype(BF16), w1,
                       preferred_element_type=F32)[0:1]
        hid = p + pltpu.roll(q, q.shape[0] - 1, axis=0) + posb
        hid = 0.5 * hid * (1.0 + jnp.tanh(math.sqrt(2.0 / math.pi) * (hid + 0.044715 * hid * hid * hid)))
        outs.append(jnp.dot(hid.astype(BF16), w2_ref[s], preferred_element_type=F32))
    o_ref[0, 0] = jnp.concatenate(outs, axis=1).astype(o_ref.dtype)


def _compress(kv_cmp, cmp_pos, cmp_w1, cmp_w2):
    B, T, _ = kv_cmp.shape
    G, d = NSA_KV_HEADS, HEAD_DIM
    n = T // CMP_STRIDE
    a = kv_cmp.reshape(B, n, CMP_STRIDE, 2, G, d).transpose(3, 0, 4, 1, 2, 5).reshape(2, B, G, n, CMP_STRIDE * d)
    hid = cmp_w1.shape[-1]
    return pl.pallas_call(
        _compress_kernel,
        grid=(B, G),
        in_specs=[
            pl.BlockSpec((2, 1, 1, n, CMP_STRIDE * d), lambda b, g: (0, b, g, 0, 0)),
            _const_spec((2, CMP_LEN * d, hid)),
            _const_spec((2, 1, CMP_LEN * d)),
            _const_spec((2, hid, d)),
        ],
        out_specs=pl.BlockSpec((1, 1, n, 2 * d), lambda b, g: (b, g, 0, 0)),
        out_shape=jax.ShapeDtypeStruct((B, G, n, 2 * d), BF16),
        compiler_params=_cparams("arbitrary", "arbitrary"),
        name="nsa_compress",
    )(a, cmp_w1.astype(BF16), cmp_pos.astype(F32).reshape(2, 1, CMP_LEN * d), cmp_w2.astype(BF16))


def _stack_heads(qb, tail):
    parts = [jnp.concatenate([qb[:, r * HEAD_DIM:(r + 1) * HEAD_DIM], tail], axis=1) for r in range(NSA_GROUP)]
    return jnp.concatenate(parts, axis=0)


def _nsa_cmp_kernel(q_ref, kc_ref, ov_ref, tri_ref, oc_ref, sel_ref, *, tq, top_k):
    qi = pl.program_id(2)
    R = NSA_GROUP
    n_cmp = kc_ref.shape[2]
    kc = kc_ref[0, 0]
    q4 = _stack_heads(q_ref[0], jnp.zeros((tq, HEAD_DIM), BF16))
    s = lax.dot_general(q4, kc, (((1,), (1,)), ((), ())), preferred_element_type=F32)
    qpos = qi * tq + lax.broadcasted_iota(jnp.int32, (tq, 1), 0)
    cmp_end = lax.broadcasted_iota(jnp.int32, (1, n_cmp), 1) * CMP_STRIDE + (CMP_LEN - 1)
    valid = cmp_end <= qpos
    valid4 = jnp.concatenate([valid] * R, axis=0)
    sm = jnp.where(valid4, s, NEG)
    e = jnp.exp2(sm - jnp.max(sm, axis=-1, keepdims=True))
    p = jnp.where(valid4, e / jnp.sum(e, axis=-1, keepdims=True), 0.0)
    oc = jnp.dot(p.astype(BF16), kc, preferred_element_type=F32)
    oc_ref[0] = jnp.concatenate([oc[r * tq:(r + 1) * tq, HEAD_DIM:] for r in range(R)], axis=1)

    psum = p[0:tq]
    for r in range(1, R):
        psum = psum + p[r * tq:(r + 1) * tq]
    ov = ov_ref[...]
    hi = psum.astype(BF16)
    r1 = psum - hi.astype(F32)
    mid = r1.astype(BF16)
    lo = (r1 - mid.astype(F32)).astype(BF16)
    imp = (jnp.dot(hi, ov, preferred_element_type=F32) + jnp.dot(mid, ov, preferred_element_type=F32)
           + jnp.dot(lo, ov, preferred_element_type=F32))
    j = lax.broadcasted_iota(jnp.int32, (1, LANES), 1)
    qblk = lax.shift_right_logical(qpos, int(math.log2(SLC_LEN)))
    forced = (j == 0) | ((j <= qblk) & (j > qblk - SLC_FORCED_LOCAL))
    imp = jnp.where(forced, FORCE_BONUS, imp)
    causal = j <= qblk
    imp = jnp.where(causal, imp, NEG)

    n_slc = tri_ref.shape[0]
    vals = imp.T[:n_slc]
    rem = vals
    cnt = jnp.zeros((1, tq), F32)
    thr = jnp.full((1, tq), NEG, F32)
    for _ in range(top_k):
        mx = jnp.max(rem, axis=0, keepdims=True)
        active = cnt < top_k
        thr = jnp.where(active, mx, thr)
        hit = rem == mx
        cnt = cnt + jnp.where(active, jnp.sum(jnp.where(hit, 1.0, 0.0), axis=0, keepdims=True), 0.0)
        rem = jnp.where(hit, BELOW_NEG, rem)
    above = vals > thr
    n_above = jnp.sum(jnp.where(above, 1.0, 0.0), axis=0, keepdims=True)
    tie = vals == thr
    prefix = jnp.dot(tri_ref[...], jnp.where(tie, 1.0, 0.0).astype(BF16), preferred_element_type=F32)
    chosen = (above | (tie & (prefix <= top_k - n_above))) & (vals > NEG)
    selb = jnp.where(chosen, 0.0, NEG)
    pad_rows = 2 * HEAD_DIM - n_slc - HEAD_DIM
    pieces = [jnp.zeros((HEAD_DIM, tq), F32), selb]
    if pad_rows:
        pieces.append(jnp.zeros((pad_rows, tq), F32))
    sel_ref[0, 0] = jnp.concatenate(pieces, axis=0).T.astype(sel_ref.dtype)


def _nsa_compressed_select(q, kc):
    B, T, D = q.shape
    G, R = NSA_KV_HEADS, NSA_GROUP
    n_cmp = kc.shape[2]
    n_slc = T // SLC_LEN
    assert n_slc <= HEAD_DIM, "selection mask is carried in the 64 spare contraction lanes"
    top_k = min(SLC_TOPK, n_slc)
    tq = min(CMP_TQ, T)
    nn = np.arange(n_cmp)[:, None]
    jj = np.arange(LANES)[None, :]
    overlap = ((nn * CMP_STRIDE < jj * SLC_LEN + SLC_LEN) & (nn * CMP_STRIDE + CMP_LEN - 1 >= jj * SLC_LEN)
               & (jj < n_slc))
    overlap = overlap & (nn < (T - CMP_LEN) // CMP_STRIDE + 1)
    tri = np.tril(np.ones((n_slc, n_slc), np.float32))
    return pl.pallas_call(
        functools.partial(_nsa_cmp_kernel, tq=tq, top_k=top_k),
        grid=(B, G, T // tq),
        in_specs=[
            pl.BlockSpec((1, tq, R * HEAD_DIM), lambda b, g, i: (b, i, g)),
            pl.BlockSpec((1, 1, n_cmp, LANES), lambda b, g, i: (b, g, 0, 0)),
            _const_spec((n_cmp, LANES)),
            _const_spec((n_slc, n_slc)),
        ],
        out_specs=[
            pl.BlockSpec((1, tq, R * HEAD_DIM), lambda b, g, i: (b, i, g)),
            pl.BlockSpec((1, 1, tq, LANES), lambda b, g, i: (b, g, i, 0)),
        ],
        out_shape=[jax.ShapeDtypeStruct((B, T, D), F32), jax.ShapeDtypeStruct((B, G, T, LANES), BF16)],
        compiler_params=_cparams("arbitrary", "arbitrary", "arbitrary"),
        name="nsa_compressed_select",
    )(q, kc, jnp.asarray(overlap.astype(np.float32), BF16), jnp.asarray(tri, BF16))


def _nsa_main_kernel(q_ref, sel_ref, ka_ref, vs_ref, kw_ref, bias_ref, cfar_ref, g_ref, oc_ref, o_ref,
                     ms_sc, ls_sc, as_sc, lw_sc, aw_sc, *, tq, tk, n_win_tiles):
    qi = pl.program_id(2)
    R = NSA_GROUP
    ratio = tk // tq
    par = qi % ratio
    jd = qi // ratio
    q4w = _stack_heads(q_ref[0], jnp.zeros((tq, HEAD_DIM), BF16))
    q4s = q4w + jnp.concatenate([sel_ref[0, 0]] * R, axis=0)

    def load_s(j):
        return (ka_ref[0, pl.ds(j * tk, tk), :], vs_ref[0, pl.ds(j * tk, tk), :])

    def load_w(j):
        kv = kw_ref[0, pl.ds(j * tk, tk), :]
        return (kv, kv)

    near_s, near_w = [], []
    for t in range(n_win_tiles):
        j = jnp.maximum(jd - t, 0)
        b = bias_ref[0, par * n_win_tiles + t]
        if t > 0:
            b = b + jnp.where(jd >= t, 0.0, NEG)
        near_w.append(load_w(j) + (b,))
        if t <= 1:
            near_s.append(load_s(j) + (b,))
    def window_branch():
        _, lw_sc[...], aw_sc[...] = _attend(q4w, near_w, _fresh_stats(R * tq))

    _causal_sweep(q4s, near_s, jnp.maximum(jd - 1, 0), load_s, cfar_ref[0], (ms_sc, ls_sc, as_sc),
                  side_work=window_branch)

    o_s = as_sc[...] / jnp.sum(ls_sc[...], axis=-1, keepdims=True)
    o_w = aw_sc[...] / jnp.sum(lw_sc[...], axis=-1, keepdims=True)
    gts = g_ref[0, 0]
    oc = oc_ref[0]
    outs = []
    for r in range(R):
        rows = slice(r * tq, (r + 1) * tq)
        o_r = (gts[:, 3 * r:3 * r + 1] * oc[:, r * HEAD_DIM:(r + 1) * HEAD_DIM]
               + gts[:, 3 * r + 1:3 * r + 2] * o_s[rows, HEAD_DIM:]
               + gts[:, 3 * r + 2:3 * r + 3] * o_w[rows, HEAD_DIM:])
        outs.append(o_r)
    o_ref[0] = jnp.concatenate(outs, axis=1).astype(o_ref.dtype)


def _nsa_main(q, sel, kv, gates, o_cmp, rel_bias):
    B, T, D = q.shape
    G, R = NSA_KV_HEADS, NSA_GROUP
    tq = min(NSA_TQ, T)
    tk = min(NSA_TK, T)
    ratio = tk // tq
    n_win_tiles = (WINDOW + tk - 1) // tk + 1
    rels = [par * tq + t * tk for par in range(ratio) for t in range(n_win_tiles)]
    assert (ratio - 1) * tq + tk + tq - 1 < WINDOW and 2 * tk - (tk - 1) >= REL_MAX_DIST
    bt = _bias_tiles(rel_bias, rels, tq, tk, WINDOW)
    n = len(rels)
    bias = bt.reshape(G, R, n, tq, tk).transpose(0, 2, 1, 3, 4).reshape(G, n, R * tq, tk)
    cfar = jnp.repeat(rel_bias.astype(F32)[REL_BUCKETS - 1].reshape(G, R) * LOG2E, tq, axis=1)
    cfar = jnp.broadcast_to(cfar.reshape(G, R * tq, 1), (G, R * tq, LANES))
    return pl.pallas_call(
        functools.partial(_nsa_main_kernel, tq=tq, tk=tk, n_win_tiles=n_win_tiles),
        grid=(B, G, T // tq),
        in_specs=[
            pl.BlockSpec((1, tq, R * HEAD_DIM), lambda b, g, i: (b, i, g)),
            pl.BlockSpec((1, 1, tq, LANES), lambda b, g, i: (b, g, i, 0)),
            pl.BlockSpec((1, T, LANES), lambda b, g, i: (b, 0, 3 * g)),
            pl.BlockSpec((1, T, LANES), lambda b, g, i: (b, 0, 3 * g + 1)),
            pl.BlockSpec((1, T, LANES), lambda b, g, i: (b, 0, 3 * g + 2)),
            pl.BlockSpec((1, n, R * tq, tk), lambda b, g, i: (g, 0, 0, 0)),
            pl.BlockSpec((1, R * tq, LANES), lambda b, g, i: (g, 0, 0)),
            pl.BlockSpec((1, 1, tq, 3 * R), lambda b, g, i: (b, g, i, 0)),
            pl.BlockSpec((1, tq, R * HEAD_DIM), lambda b, g, i: (b, i, g)),
        ],
        out_specs=pl.BlockSpec((1, tq, R * HEAD_DIM), lambda b, g, i: (b, i, g)),
        out_shape=jax.ShapeDtypeStruct((B, T, D), BF16),
        scratch_shapes=[pltpu.VMEM((R * tq, LANES), F32)] * 5,
        compiler_params=_cparams("arbitrary", "arbitrary", "arbitrary"),
        name="nsa_main",
    )(q, sel, kv, kv, kv, bias, cfar, gates, o_cmp)


def kernel(x, c, rel_bias, ada_w, ada_b, attn_norm, mlp_norm, mlp_w1, mlp_w2, a_w_in, a_w_out, a_lambda, a_subln, kv_ada_w, kv_ada_b, kv_norm, w_kv, cmp_pos, cmp_w1, cmp_w2, b_w_in, b_w_out, final_norm):
    B, T, D = x.shape
    depth = ada_w.shape[0]
    n_a = a_w_in.shape[0]
    G, R, d = NSA_KV_HEADS, NSA_GROUP, HEAD_DIM

    mod = _adaln(c, ada_w, ada_b)
    kv_mod = _adaln(c, kv_ada_w[None], kv_ada_b[None])[0]

    wkv = w_kv.reshape(D, 6, G, d)
    z = jnp.zeros((D, G, d), w_kv.dtype)
    w_cmp = wkv[:, 0:2].reshape(D, 2 * G * d).astype(BF16)
    w_sw = jnp.stack([wkv[:, 2], z, z, wkv[:, 3], wkv[:, 4], wkv[:, 5]], axis=2).reshape(D, G * 6 * d).astype(BF16)
    kv_plan = [(0, c0, w, 0, c0, "plain") for c0, w in _chunks(2 * G * d)]
    kv_plan += [(1, c0, w, 1, c0, "onehot" if (c0 // LANES) % 3 == 0 else "plain") for c0, w in _chunks(G * 6 * d, LANES)]

    shared = None
    for layer in range(depth):
        sh_a, sc_a, gt_a, sh_m, sc_m, gt_m = jnp.split(mod[layer], 6, axis=-1)
        if layer < n_a:
            w_in = a_w_in[layer].astype(BF16)
            plan = [(0, c0, w, 0, c0, "scale" if c0 < D else "plain") for c0, w in _chunks(3 * D)]
            (qkv,) = _proj(x, attn_norm[layer], sh_a, sc_a, [w_in], [(3 * D, BF16)], plan, "diff_qkv_proj")
            mix = _diff_attention(qkv, rel_bias, a_lambda[layer], a_subln[layer], layer)
            w_out = a_w_out[layer].astype(BF16)
        else:
            i = layer - n_a
            w_q = b_w_in[i][:, :D].astype(BF16)
            w_g = b_w_in[i][:, D:].astype(BF16)
            plan = [(0, c0, w, 0, c0, "scale") for c0, w in _chunks(D)] + [(1, 0, 3 * G * R, 1, 0, "sigmoid")]
            q, gates = _proj(x, attn_norm[layer], sh_a, sc_a, [w_q, w_g], [(D, BF16), (3 * G * R, F32)], plan,
                             "nsa_in_proj")
            gates = gates.reshape(B, T, G, 3 * R).transpose(0, 2, 1, 3)
            kc, kv_sw = shared
            o_cmp, sel = _nsa_compressed_select(q, kc)
            mix = _nsa_main(q, sel, kv_sw, gates, o_cmp, rel_bias)
            w_out = b_w_out[i].astype(BF16)
        x = _outproj_residual(x, mix, w_out, gt_a)
        x = _mlp_residual(x, mlp_norm[layer], sh_m, sc_m, gt_m, mlp_w1[layer].astype(BF16),
                          mlp_w2[layer].astype(BF16), final_norm, final=(layer == depth - 1))
        if layer == n_a - 1:
            sh_kv, sc_kv = jnp.split(kv_mod, 2, axis=-1)
            kv_cmp, kv_sw = _proj(x, kv_norm, sh_kv, sc_kv, [w_cmp, w_sw], [(2 * G * d, BF16), (G * 6 * d, BF16)],
                                  kv_plan, "nsa_kv_proj")
            shared = (_compress(kv_cmp, cmp_pos, cmp_w1, cmp_w2), kv_sw)
    return x
```

```python
import functools
import math

import numpy as np
import jax
import jax.numpy as jnp
from jax import lax
from jax.experimental import pallas as pl
from jax.experimental.pallas import tpu as pltpu

F32 = jnp.float32
BF16 = jnp.bfloat16

NEG = -1e30
BELOW_NEG = -3e38
NORM_EPS = 1e-6

HEAD_DIM = 64
DIFF_HEADS = 8
NSA_KV_HEADS = 4
NSA_GROUP = 4
REL_BUCKETS = 32
REL_MAX_DIST = 128
CMP_LEN = 32
CMP_STRIDE = 16
SLC_LEN = 64
SLC_TOPK = 16
SLC_FORCED_LOCAL = 2
FORCE_BONUS = 1e4
WINDOW = 512

LANES = 128
VMEM_LIMIT_BYTES = 56 * 1024 * 1024

PROJ_ROWS = 512
DIFF_TQ = 512
DIFF_TK = 256
FAR_UNROLL_LOG2 = 2
FAR_UNROLL = 1 << FAR_UNROLL_LOG2
LOG2E = math.log2(math.e)
NSA_TQ = 256
CMP_TQ = 512
FF_CHUNK = 1024


def _cparams(*sem):
    return pltpu.CompilerParams(dimension_semantics=sem, vmem_limit_bytes=VMEM_LIMIT_BYTES)


def _const_spec(shape):
    nd = len(shape)
    return pl.BlockSpec(shape, lambda *_: (0,) * nd, pipeline_mode=pl.Buffered(1))


def _adaln_kernel(c_ref, w_ref, b_ref, o_ref):
    c = c_ref[...]
    c_act = (c * (1.0 / (1.0 + jnp.exp(-c)))).astype(BF16)
    o_ref[0] = jnp.dot(c_act, w_ref[0].astype(BF16), preferred_element_type=F32) + b_ref[0]


def _adaln(c, w, b, tn=2048):
    L, D, N = w.shape
    B = c.shape[0]
    tn = min(tn, N)
    return pl.pallas_call(
        _adaln_kernel,
        grid=(L, N // tn),
        in_specs=[
            pl.BlockSpec((B, D), lambda l, j: (0, 0)),
            pl.BlockSpec((1, D, tn), lambda l, j: (l, 0, j)),
            pl.BlockSpec((1, 1, tn), lambda l, j: (l, 0, j)),
        ],
        out_specs=pl.BlockSpec((1, B, tn), lambda l, j: (l, 0, j)),
        out_shape=jax.ShapeDtypeStruct((L, B, N), F32),
        compiler_params=_cparams("arbitrary", "arbitrary"),
        name="adaln",
    )(c, w, b.reshape(L, 1, N))


def _rms_mod(x, g, shift, scale):
    ms = jnp.mean(x * x, axis=-1, keepdims=True)
    y = x * lax.rsqrt(ms + NORM_EPS) * g
    return y * (1.0 + scale) + shift


def _proj_kernel(x_ref, g_ref, sh_ref, sc_ref, *refs, n_w, plan, rows):
    w_refs, o_refs = refs[:n_w], refs[n_w:]
    h = _rms_mod(x_ref[0], g_ref[...], sh_ref[0], sc_ref[0]).astype(BF16)
    for (wi, c0, width, oi, o0, mode) in plan:
        y = jnp.dot(h, w_refs[wi][:, c0:c0 + width], preferred_element_type=F32)
        if mode in ("scale", "scale_pad"):
            y = y * (HEAD_DIM ** -0.5 * LOG2E)
        elif mode == "sigmoid":
            y = 1.0 / (1.0 + jnp.exp(-y))
        elif mode == "kv_group":
            pos = pl.program_id(1) * rows + lax.broadcasted_iota(jnp.int32, y.shape, 0)
            col = lax.broadcasted_iota(jnp.int32, y.shape, 1)
            blk = col // LANES
            upper = col % LANES - HEAD_DIM
            hot = (blk == 0) & (upper == lax.shift_right_logical(pos, int(math.log2(SLC_LEN))))
            ones = ((blk == 1) | (blk == 3)) & (upper >= 0)
            y = y + jnp.where(hot | ones, 1.0, 0.0)
        y = y.astype(o_refs[oi].dtype)
        if mode == "scale_pad":
            zeros = jnp.zeros((y.shape[0], HEAD_DIM), y.dtype)
            pieces = []
            for c in range(0, width, HEAD_DIM):
                pieces += [y[:, c:c + HEAD_DIM], zeros]
            y = jnp.concatenate(pieces, axis=1)
        o_refs[oi][0, :, o0:o0 + y.shape[1]] = y


def _proj(x, g, shift, scale, weights, outs, plan, name):
    B, T, D = x.shape
    rows = min(PROJ_ROWS, T)
    in_specs = [
        pl.BlockSpec((1, rows, D), lambda b, i: (b, i, 0)),
        _const_spec((1, D)),
        pl.BlockSpec((1, 1, D), lambda b, i: (b, 0, 0)),
        pl.BlockSpec((1, 1, D), lambda b, i: (b, 0, 0)),
    ] + [_const_spec(w.shape) for w in weights]
    out_specs = [pl.BlockSpec((1, rows, n), lambda b, i: (b, i, 0)) for n, _ in outs]
    out_shape = [jax.ShapeDtypeStruct((B, T, n), dt) for n, dt in outs]
    res = pl.pallas_call(
        functools.partial(_proj_kernel, n_w=len(weights), plan=tuple(plan), rows=rows),
        grid=(B, T // rows),
        in_specs=in_specs,
        out_specs=out_specs,
        out_shape=out_shape,
        compiler_params=_cparams("arbitrary", "arbitrary"),
        name=name,
    )(x, g.reshape(1, D), shift.reshape(B, 1, D), scale.reshape(B, 1, D), *weights)
    return res


def _chunks(n, step=512):
    return [(c, min(step, n - c)) for c in range(0, n, step)]


def _outproj_kernel(x_ref, o_ref, w_ref, gt_ref, y_ref):
    y = jnp.dot(o_ref[0], w_ref[...], preferred_element_type=F32)
    y_ref[0] = x_ref[0] + gt_ref[0] * y


def _outproj_residual(x, o, w, gate):
    B, T, D = x.shape
    rows = min(PROJ_ROWS, T)
    return pl.pallas_call(
        _outproj_kernel,
        grid=(B, T // rows),
        in_specs=[
            pl.BlockSpec((1, rows, D), lambda b, i: (b, i, 0)),
            pl.BlockSpec((1, rows, D), lambda b, i: (b, i, 0)),
            _const_spec(w.shape),
            pl.BlockSpec((1, 1, D), lambda b, i: (b, 0, 0)),
        ],
        out_specs=pl.BlockSpec((1, rows, D), lambda b, i: (b, i, 0)),
        out_shape=jax.ShapeDtypeStruct((B, T, D), F32),
        compiler_params=_cparams("arbitrary", "arbitrary"),
        name="outproj_residual",
    )(x, o, w, gate.reshape(B, 1, D))


def _mlp_kernel(x_ref, g_ref, sh_ref, sc_ref, gt_ref, w1_ref, w2_ref, fg_ref, y_ref, *, d_ff, final):
    x = x_ref[0]
    h = _rms_mod(x, g_ref[...], sh_ref[0], sc_ref[0]).astype(BF16)
    acc = jnp.zeros(x.shape, F32)
    for c0 in range(0, d_ff, FF_CHUNK):
        a = jnp.dot(h, w1_ref[:, c0:c0 + FF_CHUNK], preferred_element_type=F32)
        a = jnp.square(jnp.maximum(a, 0.0)).astype(BF16)
        acc = acc + jnp.dot(a, w2_ref[c0:c0 + FF_CHUNK, :], preferred_element_type=F32)
    y = x + gt_ref[0] * acc
    if final:
        ms = jnp.mean(y * y, axis=-1, keepdims=True)
        y = y * lax.rsqrt(ms + NORM_EPS) * fg_ref[...]
    y_ref[0] = y


def _mlp_residual(x, g, shift, scale, gate, w1, w2, final_gain, final):
    B, T, D = x.shape
    d_ff = w1.shape[1]
    rows = min(PROJ_ROWS, T)
    vec = pl.BlockSpec((1, 1, D), lambda b, i: (b, 0, 0))
    return pl.pallas_call(
        functools.partial(_mlp_kernel, d_ff=d_ff, final=final),
        grid=(B, T // rows),
        in_specs=[
            pl.BlockSpec((1, rows, D), lambda b, i: (b, i, 0)),
            _const_spec((1, D)), vec, vec, vec,
            _const_spec(w1.shape), _const_spec(w2.shape),
            _const_spec((1, D)),
        ],
        out_specs=pl.BlockSpec((1, rows, D), lambda b, i: (b, i, 0)),
        out_shape=jax.ShapeDtypeStruct((B, T, D), F32),
        compiler_params=_cparams("arbitrary", "arbitrary"),
        name="mlp_residual",
    )(x, g.reshape(1, D), shift.reshape(B, 1, D), scale.reshape(B, 1, D), gate.reshape(B, 1, D),
      w1, w2, final_gain.reshape(1, D))


def _t5_bucket(dist):
    n = jnp.maximum(dist, 0)
    max_exact = REL_BUCKETS // 2
    nf = jnp.maximum(n, 1).astype(F32)
    large = max_exact + (jnp.log(nf / max_exact) / math.log(REL_MAX_DIST / max_exact)
                         * (REL_BUCKETS - max_exact)).astype(jnp.int32)
    large = jnp.minimum(large, REL_BUCKETS - 1)
    return jnp.where(n < max_exact, n, large)


def _bias_tiles(rel_bias, rels, tq, tk, window):
    iq = np.arange(tq)[:, None]
    ik = np.arange(tk)[None, :]
    dist = jnp.asarray(np.stack([r + iq - ik for r in rels]).astype(np.int32))
    hot = (_t5_bucket(dist)[..., None] == jnp.arange(REL_BUCKETS)).astype(F32)
    b = jnp.einsum("ntkb,bh->hntk", hot, rel_bias.astype(F32), precision=lax.Precision.HIGHEST)
    ok = dist >= 0
    if window is not None:
        ok = ok & (dist < window)
    tiles = jnp.where(ok[None], b * LOG2E, NEG)
    return jnp.concatenate([tiles, jnp.full_like(tiles[:, :1], NEG)], axis=1)


def _attend(q, tiles, stats):
    m, l, acc = stats
    for tile in tiles:
        k, v, bias = tile()
        s = lax.dot_general(q, k, (((1,), (1,)), ((), ())), preferred_element_type=F32)
        parts = [s[:, i:i + LANES] for i in range(0, s.shape[1], LANES)]
        if bias is not None:
            cols = [bias[:, i:i + LANES] for i in range(0, bias.shape[1], LANES)]
            parts = [t + cols[i % len(cols)] for i, t in enumerate(parts)]
        mx = functools.reduce(jnp.maximum, parts)
        m_new = jnp.maximum(m, jnp.max(mx, axis=-1, keepdims=True))
        alpha = jnp.exp2(m - m_new)
        p = [jnp.exp2(t - m_new) for t in parts]
        if l is not None:
            l = alpha * l + functools.reduce(jnp.add, p)
        pv = jnp.dot(jnp.concatenate(p, axis=1).astype(BF16), v, preferred_element_type=F32)
        acc = alpha * acc + pv
        m = m_new
    return m, l, acc


def _fresh_stats(rows, row_sums=True):
    zeros = jnp.zeros((rows, LANES), F32)
    return (jnp.full((rows, LANES), NEG, F32), zeros if row_sums else None, zeros)


def _causal_sweep(q, near, n_far, load_tile, cfar, scratch, side_work=None):
    m_sc, l_sc, acc_sc = scratch
    rows = q.shape[0]
    n_groups = lax.shift_right_logical(n_far, FAR_UNROLL_LOG2)
    rem = n_far - n_groups * FAR_UNROLL

    def save(m, l, acc):
        m_sc[...], acc_sc[...] = m, acc
        if l_sc is not None:
            l_sc[...] = l

    for r in range(FAR_UNROLL):
        @pl.when(rem == r)
        def _():
            extra = [functools.partial(load_tile, n_groups * FAR_UNROLL + i, cfar) for i in range(r)]
            m, l, acc = _attend(q, near + extra, _fresh_stats(rows, l_sc is not None))
            save(m - cfar(), l, acc)
            if side_work is not None:
                side_work()

    def group(g, carry):
        tiles = [functools.partial(load_tile, g * FAR_UNROLL + u, None) for u in range(FAR_UNROLL)]
        save(*_attend(q, tiles, (m_sc[...], None if l_sc is None else l_sc[...], acc_sc[...])))
        return carry

    lax.fori_loop(0, n_groups, group, 0)


def _diff_attn_kernel(q_ref, k_ref, v_ref, bias_ref, cfar_ref, lam_ref, subln_ref, o_ref,
                      m_sc, l_sc, acc_sc, *, tq, tk, lam_init):
    qi = pl.program_id(2)
    q = q_ref[0]
    lane = lax.broadcasted_iota(jnp.int32, q.shape, 1)
    zero = jnp.zeros_like(q)
    q2 = jnp.concatenate([jnp.where(lane < HEAD_DIM, q, zero),
                          jnp.where(lane >= HEAD_DIM, q, zero)], axis=0)

    def load_tile(j, bias):
        return (k_ref[0, pl.ds(j * tk, tk), :], v_ref[0, pl.ds(j * tk, tk), :], bias() if bias else None)

    ratio = tq // tk
    j0 = qi * ratio
    near = [functools.partial(load_tile, j0 + r, lambda r=r: bias_ref[0, r]) for r in range(ratio)]
    near.append(functools.partial(load_tile, jnp.maximum(j0 - 1, 0),
                                  lambda: bias_ref[0, jnp.where(qi >= 1, ratio, ratio + 1)]))
    _causal_sweep(q2, near, jnp.maximum(j0 - 1, 0), load_tile, lambda: cfar_ref[0], (m_sc, l_sc, acc_sc))

    lf = lam_ref[...]
    lam = (jnp.exp(jnp.sum(lf[0:1] * lf[1:2], axis=-1, keepdims=True))
           - jnp.exp(jnp.sum(lf[2:3] * lf[3:4], axis=-1, keepdims=True)) + lam_init)
    on = acc_sc[...] / jnp.sum(l_sc[...], axis=-1, keepdims=True)
    o = on[:tq] - lam * on[tq:]
    ms = jnp.mean(o * o, axis=-1, keepdims=True)
    o = o * lax.rsqrt(ms + NORM_EPS) * subln_ref[...] * (1.0 - lam_init)
    o_ref[0] = o.astype(o_ref.dtype)


def _diff_attention(qkv, rel_bias, lam, subln, layer_idx):
    B, T, D3 = qkv.shape
    D = D3 // 3
    H = DIFF_HEADS
    tq = min(DIFF_TQ, T)
    tk = min(DIFF_TK, tq)
    ratio = tq // tk
    lam_init = 0.8 - 0.6 * math.exp(-0.3 * layer_idx)
    assert 2 * tk - (tk - 1) >= REL_MAX_DIST
    rels = [-r * tk for r in range(ratio)] + [tk]
    nb = len(rels) + 1
    bt = _bias_tiles(rel_bias, rels, tq, tk, None)
    bias = bt.reshape(H, 2, nb, tq, tk).transpose(0, 2, 1, 3, 4).reshape(H, nb, 2 * tq, tk)
    cfar = jnp.repeat(rel_bias.astype(F32)[REL_BUCKETS - 1].reshape(H, 2) * LOG2E, tq, axis=1)
    cfar = jnp.broadcast_to(cfar.reshape(H, 2 * tq, 1), (H, 2 * tq, LANES))
    hb = D // LANES
    return pl.pallas_call(
        functools.partial(_diff_attn_kernel, tq=tq, tk=tk, lam_init=lam_init),
        grid=(B, H, T // tq),
        in_specs=[
            pl.BlockSpec((1, tq, LANES), lambda b, h, i: (b, i, h)),
            pl.BlockSpec((1, T, LANES), lambda b, h, i: (b, 0, hb + h)),
            pl.BlockSpec((1, T, LANES), lambda b, h, i: (b, 0, 2 * hb + h)),
            pl.BlockSpec((1, nb, 2 * tq, tk), lambda b, h, i: (h, 0, 0, 0)),
            pl.BlockSpec((1, 2 * tq, LANES), lambda b, h, i: (h, 0, 0)),
            _const_spec(lam.shape),
            _const_spec((1, LANES)),
        ],
        out_specs=pl.BlockSpec((1, tq, LANES), lambda b, h, i: (b, i, h)),
        out_shape=jax.ShapeDtypeStruct((B, T, D), BF16),
        scratch_shapes=[pltpu.VMEM((2 * tq, LANES), F32)] * 3,
        compiler_params=_cparams("arbitrary", "arbitrary", "arbitrary"),
        name="diff_attention",
    )(qkv, qkv, qkv, bias, cfar, lam.astype(F32), subln.astype(F32).reshape(1, LANES))


def _compress_kernel(a_ref, w1_ref, pos_ref, w2_ref, o_ref):
    half = CMP_STRIDE * HEAD_DIM
    for s in range(2):
        a = a_ref[s, 0, 0]
        w1 = w1_ref[s]
        p = jnp.dot(a, w1[:half], preferred_element_type=F32)
        q = jnp.dot(a, w1[half:], preferred_element_type=F32)
        posb = jnp.dot(jnp.broadcast_to(pos_ref[s], (8, 2 * half)).astype(BF16), w1,
                       preferred_element_type=F32)[0:1]
        hid = p + pltpu.roll(q, q.shape[0] - 1, axis=0) + posb
        hid = 0.5 * hid * (1.0 + jnp.tanh(math.sqrt(2.0 / math.pi) * (hid + 0.044715 * hid * hid * hid)))
        c = jnp.dot(hid.astype(BF16), w2_ref[s], preferred_element_type=F32)
        o_ref[0, 0, s] = jnp.concatenate([c, jnp.zeros_like(c)], axis=1).astype(o_ref.dtype)


def _compress(kv_cmp, cmp_pos, cmp_w1, cmp_w2):
    B, T, _ = kv_cmp.shape
    G, d = NSA_KV_HEADS, HEAD_DIM
    n = T // CMP_STRIDE
    a = kv_cmp.reshape(B, n, CMP_STRIDE, 2, G, d).transpose(3, 0, 4, 1, 2, 5).reshape(2, B, G, n, CMP_STRIDE * d)
    hid = cmp_w1.shape[-1]
    return pl.pallas_call(
        _compress_kernel,
        grid=(B, G),
        in_specs=[
            pl.BlockSpec((2, 1, 1, n, CMP_STRIDE * d), lambda b, g: (0, b, g, 0, 0)),
            _const_spec((2, CMP_LEN * d, hid)),
            _const_spec((2, 1, CMP_LEN * d)),
            _const_spec((2, hid, d)),
        ],
        out_specs=pl.BlockSpec((1, 1, 2, n, LANES), lambda b, g: (b, g, 0, 0, 0)),
        out_shape=jax.ShapeDtypeStruct((B, G, 2, n, LANES), BF16),
        compiler_params=_cparams("arbitrary", "arbitrary"),
        name="nsa_compress",
    )(a, cmp_w1.astype(BF16), cmp_pos.astype(F32).reshape(2, 1, CMP_LEN * d), cmp_w2.astype(BF16))


def _rows_from_lane_blocks(x):
    return jnp.concatenate([x[:, r * LANES:(r + 1) * LANES] for r in range(NSA_GROUP)], axis=0)


def _lane_blocks_from_rows(x):
    tq = x.shape[0] // NSA_GROUP
    return jnp.concatenate([x[r * tq:(r + 1) * tq] for r in range(NSA_GROUP)], axis=1)


def _nsa_cmp_kernel(q_ref, kc_ref, ov_ref, tri_ref, oc_ref, sel_ref, *, tq, top_k):
    qi = pl.program_id(2)
    R = NSA_GROUP
    n_cmp = kc_ref.shape[3]
    k_c = kc_ref[0, 0, 0]
    v_c = kc_ref[0, 0, 1]
    q4 = _rows_from_lane_blocks(q_ref[0])
    s = lax.dot_general(q4, k_c, (((1,), (1,)), ((), ())), preferred_element_type=F32)
    qpos = qi * tq + lax.broadcasted_iota(jnp.int32, (tq, 1), 0)
    cmp_end = lax.broadcasted_iota(jnp.int32, (1, n_cmp), 1) * CMP_STRIDE + (CMP_LEN - 1)
    valid4 = jnp.concatenate([cmp_end <= qpos] * R, axis=0)
    sm = jnp.where(valid4, s, NEG)
    e = jnp.exp2(sm - jnp.max(sm, axis=-1, keepdims=True))
    any_valid = jnp.concatenate([qpos >= CMP_LEN - 1] * R, axis=0)
    p = e * jnp.where(any_valid, 1.0 / jnp.sum(e, axis=-1, keepdims=True), 0.0)
    oc = jnp.dot(p.astype(BF16), v_c, preferred_element_type=F32)
    oc_ref[0] = _lane_blocks_from_rows(oc).astype(oc_ref.dtype)

    psum = p[0:tq]
    for r in range(1, R):
        psum = psum + p[r * tq:(r + 1) * tq]
    ov = ov_ref[...]
    hi = psum.astype(BF16)
    r1 = psum - hi.astype(F32)
    mid = r1.astype(BF16)
    lo = (r1 - mid.astype(F32)).astype(BF16)
    imp = (jnp.dot(hi, ov, preferred_element_type=F32) + jnp.dot(mid, ov, preferred_element_type=F32)
           + jnp.dot(lo, ov, preferred_element_type=F32))
    j = lax.broadcasted_iota(jnp.int32, (1, LANES), 1)
    qblk = lax.shift_right_logical(qpos, int(math.log2(SLC_LEN)))
    forced = (j == 0) | ((j <= qblk) & (j > qblk - SLC_FORCED_LOCAL))
    imp = jnp.where(forced, FORCE_BONUS, imp)
    imp = jnp.where(j <= qblk, imp, NEG)

    n_slc = tri_ref.shape[0]
    vals = imp.T[:n_slc]
    rem = vals
    cnt = jnp.zeros((1, tq), F32)
    thr = jnp.full((1, tq), NEG, F32)
    for _ in range(top_k):
        mx = jnp.max(rem, axis=0, keepdims=True)
        active = cnt < top_k
        thr = jnp.where(active, mx, thr)
        hit = rem == mx
        cnt = cnt + jnp.where(active, jnp.sum(jnp.where(hit, 1.0, 0.0), axis=0, keepdims=True), 0.0)
        rem = jnp.where(hit, BELOW_NEG, rem)
    above = vals > thr
    n_above = jnp.sum(jnp.where(above, 1.0, 0.0), axis=0, keepdims=True)
    tie = vals == thr
    prefix = jnp.dot(tri_ref[...], jnp.where(tie, 1.0, 0.0).astype(BF16), preferred_element_type=F32)
    chosen = (above | (tie & (prefix <= top_k - n_above))) & (vals > NEG)
    selb = jnp.where(chosen, 0.0, NEG)
    pieces = [jnp.zeros((HEAD_DIM, tq), F32), selb]
    if HEAD_DIM - n_slc:
        pieces.append(jnp.zeros((HEAD_DIM - n_slc, tq), F32))
    sel_ref[0, 0] = jnp.concatenate(pieces, axis=0).T.astype(sel_ref.dtype)


def _nsa_compressed_select(q, kc):
    B, T, _ = q.shape
    G, R = NSA_KV_HEADS, NSA_GROUP
    n_cmp = kc.shape[3]
    n_slc = T // SLC_LEN
    assert n_slc <= HEAD_DIM, "selection mask is carried in the 64 spare contraction lanes"
    top_k = min(SLC_TOPK, n_slc)
    tq = min(CMP_TQ, T)
    nn = np.arange(n_cmp)[:, None]
    jj = np.arange(LANES)[None, :]
    overlap = ((nn * CMP_STRIDE < jj * SLC_LEN + SLC_LEN) & (nn * CMP_STRIDE + CMP_LEN - 1 >= jj * SLC_LEN)
               & (jj < n_slc))
    overlap = overlap & (nn < (T - CMP_LEN) // CMP_STRIDE + 1)
    tri = np.tril(np.ones((n_slc, n_slc), np.float32))
    return pl.pallas_call(
        functools.partial(_nsa_cmp_kernel, tq=tq, top_k=top_k),
        grid=(B, G, T // tq),
        in_specs=[
            pl.BlockSpec((1, tq, R * LANES), lambda b, g, i: (b, i, g)),
            pl.BlockSpec((1, 1, 2, n_cmp, LANES), lambda b, g, i: (b, g, 0, 0, 0)),
            _const_spec((n_cmp, LANES)),
            _const_spec((n_slc, n_slc)),
        ],
        out_specs=[
            pl.BlockSpec((1, tq, R * LANES), lambda b, g, i: (b, i, g)),
            pl.BlockSpec((1, 1, tq, LANES), lambda b, g, i: (b, g, i, 0)),
        ],
        out_shape=[jax.ShapeDtypeStruct((B, T, G * R * LANES), BF16), jax.ShapeDtypeStruct((B, G, T, LANES), BF16)],
        compiler_params=_cparams("arbitrary", "arbitrary", "arbitrary"),
        name="nsa_compressed_select",
    )(q, kc, jnp.asarray(overlap.astype(np.float32), BF16), jnp.asarray(tri, BF16))


def _nsa_main_kernel(q_ref, sel_ref, ka_ref, vs_ref, kw_ref, vw_ref, bias_ref, cfar_ref, g_ref, ge_ref, cp_ref,
                     oc_ref, o_ref, ms_sc, as_sc, aw_sc, *, tq, tk, n_win):
    qi = pl.program_id(2)
    R = NSA_GROUP
    q4w = _rows_from_lane_blocks(q_ref[0])
    q4s = q4w + jnp.concatenate([sel_ref[0, 0]] * R, axis=0)

    def load_s(j, bias):
        return (ka_ref[0, pl.ds(j * tk, tk), :], vs_ref[0, pl.ds(j * tk, tk), :], bias() if bias else None)

    def load_w(j, bias):
        return (kw_ref[0, pl.ds(j * tk, tk), :], vw_ref[0, pl.ds(j * tk, tk), :], bias() if bias else None)

    near_s, near_w = [], []
    for t in range(n_win):
        j = jnp.maximum(qi - t, 0)
        b = (lambda: bias_ref[0, 0]) if t == 0 else (lambda t=t: bias_ref[0, jnp.where(qi >= t, t, n_win)])
        near_w.append(functools.partial(load_w, j, b))
        if t <= 1:
            near_s.append(functools.partial(load_s, j, b))

    def window_branch():
        _, _, aw_sc[...] = _attend(q4w, near_w, _fresh_stats(R * tq, row_sums=False))

    _causal_sweep(q4s, near_s, jnp.maximum(qi - 1, 0), load_s, lambda: cfar_ref[0], (ms_sc, None, as_sc),
                  side_work=window_branch)

    def normalised(acc):
        return acc / jnp.max(acc[:, HEAD_DIM:], axis=-1, keepdims=True)

    o_s = _lane_blocks_from_rows(normalised(as_sc[...]))
    o_w = _lane_blocks_from_rows(normalised(aw_sc[...]))
    g = g_ref[0]
    lane = lax.broadcasted_iota(jnp.int32, g.shape, 1)
    hi = g.astype(BF16)
    r1 = g - hi.astype(F32)
    mid = r1.astype(BF16)
    lo = (r1 - mid.astype(F32)).astype(BF16)
    n_g = 3 * R
    g_split = jnp.where(lane < n_g, hi, jnp.where(lane < 2 * n_g, mid, lo))
    ge = jnp.dot(g_split, ge_ref[...], preferred_element_type=F32)
    w = R * LANES
    o = ge[:, :w] * oc_ref[0].astype(F32) + ge[:, w:2 * w] * o_s + ge[:, 2 * w:] * o_w
    o_ref[0] = jnp.dot(o.astype(BF16), cp_ref[...], preferred_element_type=F32).astype(o_ref.dtype)


def _nsa_main(q, sel, kv, gates, o_cmp, rel_bias):
    B, T, _ = q.shape
    G, R = NSA_KV_HEADS, NSA_GROUP
    D = G * R * HEAD_DIM
    tq = min(NSA_TQ, T)
    tk = tq
    n_win = (WINDOW - 2) // tk + 2
    rels = [t * tk for t in range(n_win)]
    assert tk + tq - 1 < WINDOW and 2 * tk - (tk - 1) >= REL_MAX_DIST
    bt = _bias_tiles(rel_bias, rels, tq, tk, WINDOW)
    nb = n_win + 1
    bias = bt.reshape(G, R, nb, tq, tk).transpose(0, 2, 1, 3, 4).reshape(G, nb, R * tq, tk)
    cfar = jnp.repeat(rel_bias.astype(F32)[REL_BUCKETS - 1].reshape(G, R) * LOG2E, tq, axis=1)
    cfar = jnp.broadcast_to(cfar.reshape(G, R * tq, 1), (G, R * tq, LANES))
    ge = np.zeros((LANES, 3 * R * LANES), np.float32)
    for rep in range(3):
        for r in range(R):
            for br in range(3):
                c0 = br * R * LANES + r * LANES
                ge[rep * 3 * R + 3 * r + br, c0:c0 + HEAD_DIM] = 1.0
    cp = np.zeros((R * LANES, R * HEAD_DIM), np.float32)
    for r in range(R):
        cp[r * LANES + np.arange(HEAD_DIM), r * HEAD_DIM + np.arange(HEAD_DIM)] = 1.0
    blk = lambda k: pl.BlockSpec((1, T, LANES), lambda b, g, i: (b, 0, 4 * g + k))
    return pl.pallas_call(
        functools.partial(_nsa_main_kernel, tq=tq, tk=tk, n_win=n_win),
        grid=(B, G, T // tq),
        in_specs=[
            pl.BlockSpec((1, tq, R * LANES), lambda b, g, i: (b, i, g)),
            pl.BlockSpec((1, 1, tq, LANES), lambda b, g, i: (b, g, i, 0)),
            blk(0), blk(1), blk(2), blk(3),
            pl.BlockSpec((1, nb, R * tq, tk), lambda b, g, i: (g, 0, 0, 0)),
            pl.BlockSpec((1, R * tq, LANES), lambda b, g, i: (g, 0, 0)),
            pl.BlockSpec((1, tq, LANES), lambda b, g, i: (b, i, g)),
            _const_spec(ge.shape),
            _const_spec(cp.shape),
            pl.BlockSpec((1, tq, R * LANES), lambda b, g, i: (b, i, g)),
        ],
        out_specs=pl.BlockSpec((1, tq, R * HEAD_DIM), lambda b, g, i: (b, i, g)),
        out_shape=jax.ShapeDtypeStruct((B, T, D), BF16),
        scratch_shapes=[pltpu.VMEM((R * tq, LANES), F32)] * 3,
        compiler_params=_cparams("arbitrary", "arbitrary", "arbitrary"),
        name="nsa_main",
    )(q, sel, kv, kv, kv, kv, bias, cfar, gates, jnp.asarray(ge, BF16), jnp.asarray(cp, BF16), o_cmp)


def kernel(x, c, rel_bias, ada_w, ada_b, attn_norm, mlp_norm, mlp_w1, mlp_w2, a_w_in, a_w_out, a_lambda, a_subln, kv_ada_w, kv_ada_b, kv_norm, w_kv, cmp_pos, cmp_w1, cmp_w2, b_w_in, b_w_out, final_norm):
    B, T, D = x.shape
    depth = ada_w.shape[0]
    n_a = a_w_in.shape[0]
    G, R, d = NSA_KV_HEADS, NSA_GROUP, HEAD_DIM

    mod = _adaln(c, ada_w, ada_b)
    kv_mod = _adaln(c, kv_ada_w[None], kv_ada_b[None])[0]

    wkv = w_kv.reshape(D, 6, G, d)
    z = jnp.zeros((D, G, d), w_kv.dtype)
    w_cmp = wkv[:, 0:2].reshape(D, 2 * G * d).astype(BF16)
    w_sw = jnp.stack([wkv[:, 2], z, wkv[:, 3], z, wkv[:, 4], wkv[:, 5], wkv[:, 5], z], axis=2)
    w_sw = w_sw.reshape(D, G * 4 * LANES).astype(BF16)
    kv_plan = [(0, c0, w, 0, c0, "plain") for c0, w in _chunks(2 * G * d)]
    kv_plan += [(1, c0, w, 1, c0, "kv_group") for c0, w in _chunks(G * 4 * LANES, 4 * LANES)]

    shared = None
    for layer in range(depth):
        sh_a, sc_a, gt_a, sh_m, sc_m, gt_m = jnp.split(mod[layer], 6, axis=-1)
        if layer < n_a:
            w_in = a_w_in[layer].astype(BF16)
            plan = [(0, c0, w, 0, c0, "scale" if c0 < D else "plain") for c0, w in _chunks(3 * D)]
            (qkv,) = _proj(x, attn_norm[layer], sh_a, sc_a, [w_in], [(3 * D, BF16)], plan, "diff_qkv_proj")
            mix = _diff_attention(qkv, rel_bias, a_lambda[layer], a_subln[layer], layer)
            w_out = a_w_out[layer].astype(BF16)
        else:
            i = layer - n_a
            w_q = b_w_in[i][:, :D].astype(BF16)
            w_g = jnp.tile(b_w_in[i][:, D:].reshape(D, G, 3 * R), (1, 1, 3))
            w_g = jnp.pad(w_g, ((0, 0), (0, 0), (0, LANES - 9 * R))).reshape(D, G * LANES).astype(BF16)
            plan = [(0, c0, w, 0, 2 * c0, "scale_pad") for c0, w in _chunks(D)]
            plan += [(1, 0, G * LANES, 1, 0, "sigmoid")]
            q, gates = _proj(x, attn_norm[layer], sh_a, sc_a, [w_q, w_g], [(2 * D, BF16), (G * LANES, F32)], plan,
                             "nsa_in_proj")
            kc, kv_sw = shared
            o_cmp, sel = _nsa_compressed_select(q, kc)
            mix = _nsa_main(q, sel, kv_sw, gates, o_cmp, rel_bias)
            w_out = b_w_out[i].astype(BF16)
        x = _outproj_residual(x, mix, w_out, gt_a)
        x = _mlp_residual(x, mlp_norm[layer], sh_m, sc_m, gt_m, mlp_w1[layer].astype(BF16),
                          mlp_w2[layer].astype(BF16), final_norm, final=(layer == depth - 1))
        if layer == n_a - 1:
            sh_kv, sc_kv = jnp.split(kv_mod, 2, axis=-1)
            kv_cmp, kv_sw = _proj(x, kv_norm, sh_kv, sc_kv, [w_cmp, w_sw], [(2 * G * d, BF16), (G * 4 * LANES, BF16)],
                                  kv_plan, "nsa_kv_proj")
            shared = (_compress(kv_cmp, cmp_pos, cmp_w1, cmp_w2), kv_sw)
    return x
```

```python
import functools
import math

import numpy as np
import jax
import jax.numpy as jnp
from jax import lax
from jax.experimental import pallas as pl
from jax.experimental.pallas import tpu as pltpu

F32 = jnp.float32
BF16 = jnp.bfloat16

NEG = -1e30
BELOW_NEG = -3e38
NORM_EPS = 1e-6

HEAD_DIM = 64
DIFF_HEADS = 8
NSA_KV_HEADS = 4
NSA_GROUP = 4
REL_BUCKETS = 32
REL_MAX_DIST = 128
CMP_LEN = 32
CMP_STRIDE = 16
SLC_LEN = 64
SLC_TOPK = 16
SLC_FORCED_LOCAL = 2
FORCE_BONUS = 1e4
WINDOW = 512

LANES = 128
VMEM_LIMIT_BYTES = 56 * 1024 * 1024

PROJ_ROWS = 512
DIFF_TQ = 512
DIFF_TK = 256
FAR_UNROLL_LOG2 = 2
FAR_UNROLL = 1 << FAR_UNROLL_LOG2
LOG2E = math.log2(math.e)
NSA_TQ = 256
CMP_TQ = 512
FF_CHUNK = 1024


def _cparams(*sem):
    return pltpu.CompilerParams(dimension_semantics=sem, vmem_limit_bytes=VMEM_LIMIT_BYTES)


def _const_spec(shape):
    nd = len(shape)
    return pl.BlockSpec(shape, lambda *_: (0,) * nd, pipeline_mode=pl.Buffered(1))


def _adaln_kernel(c_ref, w_ref, b_ref, o_ref):
    c = c_ref[...]
    c_act = (c * (1.0 / (1.0 + jnp.exp(-c)))).astype(BF16)
    o_ref[0] = jnp.dot(c_act, w_ref[0].astype(BF16), preferred_element_type=F32) + b_ref[0]


def _adaln(c, w, b, tn=2048):
    L, D, N = w.shape
    B = c.shape[0]
    tn = min(tn, N)
    return pl.pallas_call(
        _adaln_kernel,
        grid=(L, N // tn),
        in_specs=[
            pl.BlockSpec((B, D), lambda l, j: (0, 0)),
            pl.BlockSpec((1, D, tn), lambda l, j: (l, 0, j)),
            pl.BlockSpec((1, 1, tn), lambda l, j: (l, 0, j)),
        ],
        out_specs=pl.BlockSpec((1, B, tn), lambda l, j: (l, 0, j)),
        out_shape=jax.ShapeDtypeStruct((L, B, N), F32),
        compiler_params=_cparams("arbitrary", "arbitrary"),
        name="adaln",
    )(c, w, b.reshape(L, 1, N))


def _rms_mod(x, g, shift, scale):
    ms = jnp.mean(x * x, axis=-1, keepdims=True)
    y = x * lax.rsqrt(ms + NORM_EPS) * g
    return y * (1.0 + scale) + shift


def _proj_kernel(x_ref, g_ref, sh_ref, sc_ref, *refs, n_w, plan, rows):
    w_refs, o_refs = refs[:n_w], refs[n_w:]
    h = _rms_mod(x_ref[0], g_ref[...], sh_ref[0], sc_ref[0]).astype(BF16)
    for (wi, c0, width, oi, o0, mode) in plan:
        y = jnp.dot(h, w_refs[wi][:, c0:c0 + width], preferred_element_type=F32)
        if mode in ("scale", "scale_pad"):
            y = y * (HEAD_DIM ** -0.5 * LOG2E)
        elif mode == "sigmoid":
            y = 1.0 / (1.0 + jnp.exp(-y))
        elif mode == "kv_group":
            pos = pl.program_id(1) * rows + lax.broadcasted_iota(jnp.int32, y.shape, 0)
            col = lax.broadcasted_iota(jnp.int32, y.shape, 1)
            blk = col // LANES
            upper = col % LANES - HEAD_DIM
            hot = (blk == 0) & (upper == lax.shift_right_logical(pos, int(math.log2(SLC_LEN))))
            ones = ((blk == 1) | (blk == 3)) & (upper >= 0)
            y = y + jnp.where(hot | ones, 1.0, 0.0)
        y = y.astype(o_refs[oi].dtype)
        if mode == "scale_pad":
            zeros = jnp.zeros((y.shape[0], HEAD_DIM), y.dtype)
            pieces = []
            for c in range(0, width, HEAD_DIM):
                pieces += [y[:, c:c + HEAD_DIM], zeros]
            y = jnp.concatenate(pieces, axis=1)
        o_refs[oi][0, :, o0:o0 + y.shape[1]] = y


def _proj(x, g, shift, scale, weights, outs, plan, name):
    B, T, D = x.shape
    rows = min(PROJ_ROWS, T)
    in_specs = [
        pl.BlockSpec((1, rows, D), lambda b, i: (b, i, 0)),
        _const_spec((1, D)),
        pl.BlockSpec((1, 1, D), lambda b, i: (b, 0, 0)),
        pl.BlockSpec((1, 1, D), lambda b, i: (b, 0, 0)),
    ] + [_const_spec(w.shape) for w in weights]
    out_specs = [pl.BlockSpec((1, rows, n), lambda b, i: (b, i, 0)) for n, _ in outs]
    out_shape = [jax.ShapeDtypeStruct((B, T, n), dt) for n, dt in outs]
    res = pl.pallas_call(
        functools.partial(_proj_kernel, n_w=len(weights), plan=tuple(plan), rows=rows),
        grid=(B, T // rows),
        in_specs=in_specs,
        out_specs=out_specs,
        out_shape=out_shape,
        compiler_params=_cparams("arbitrary", "arbitrary"),
        name=name,
    )(x, g.reshape(1, D), shift.reshape(B, 1, D), scale.reshape(B, 1, D), *weights)
    return res


def _chunks(n, step=512):
    return [(c, min(step, n - c)) for c in range(0, n, step)]


def _mix_mlp_kernel(x_ref, o_ref, wo_ref, ga_ref, g_ref, sh_ref, sc_ref, gm_ref, w1_ref, w2_ref, fg_ref, y_ref,
                    *, d_ff, final):
    x = x_ref[0] + ga_ref[0] * jnp.dot(o_ref[0], wo_ref[...], preferred_element_type=F32)
    h = _rms_mod(x, g_ref[...], sh_ref[0], sc_ref[0]).astype(BF16)
    acc = jnp.zeros(x.shape, F32)
    for c0 in range(0, d_ff, FF_CHUNK):
        a = jnp.dot(h, w1_ref[:, c0:c0 + FF_CHUNK], preferred_element_type=F32)
        a = jnp.square(jnp.maximum(a, 0.0)).astype(BF16)
        acc = acc + jnp.dot(a, w2_ref[c0:c0 + FF_CHUNK, :], preferred_element_type=F32)
    y = x + gm_ref[0] * acc
    if final:
        ms = jnp.mean(y * y, axis=-1, keepdims=True)
        y = y * lax.rsqrt(ms + NORM_EPS) * fg_ref[...]
    y_ref[0] = y


def _mix_mlp_residual(x, o, w_out, gate_a, g, shift, scale, gate_m, w1, w2, final_gain, final):
    B, T, D = x.shape
    d_ff = w1.shape[1]
    rows = min(PROJ_ROWS, T)
    tile = pl.BlockSpec((1, rows, D), lambda b, i: (b, i, 0))
    vec = pl.BlockSpec((1, 1, D), lambda b, i: (b, 0, 0))
    per_batch = lambda v: v.reshape(B, 1, D)
    return pl.pallas_call(
        functools.partial(_mix_mlp_kernel, d_ff=d_ff, final=final),
        grid=(B, T // rows),
        in_specs=[tile, tile, _const_spec(w_out.shape), vec,
                  _const_spec((1, D)), vec, vec, vec,
                  _const_spec(w1.shape), _const_spec(w2.shape), _const_spec((1, D))],
        out_specs=tile,
        out_shape=jax.ShapeDtypeStruct((B, T, D), F32),
        compiler_params=_cparams("arbitrary", "arbitrary"),
        name="mix_mlp_residual",
    )(x, o, w_out, per_batch(gate_a), g.reshape(1, D), per_batch(shift), per_batch(scale), per_batch(gate_m),
      w1, w2, final_gain.reshape(1, D))


def _t5_bucket(dist):
    n = jnp.maximum(dist, 0)
    max_exact = REL_BUCKETS // 2
    nf = jnp.maximum(n, 1).astype(F32)
    large = max_exact + (jnp.log(nf / max_exact) / math.log(REL_MAX_DIST / max_exact)
                         * (REL_BUCKETS - max_exact)).astype(jnp.int32)
    large = jnp.minimum(large, REL_BUCKETS - 1)
    return jnp.where(n < max_exact, n, large)


def _bias_tiles(rel_bias, rels, tq, tk, window):
    iq = np.arange(tq)[:, None]
    ik = np.arange(tk)[None, :]
    dist = jnp.asarray(np.stack([r + iq - ik for r in rels]).astype(np.int32))
    hot = (_t5_bucket(dist)[..., None] == jnp.arange(REL_BUCKETS)).astype(F32)
    b = jnp.einsum("ntkb,bh->hntk", hot, rel_bias.astype(F32), precision=lax.Precision.HIGHEST)
    ok = dist >= 0
    if window is not None:
        ok = ok & (dist < window)
    tiles = jnp.where(ok[None], b * LOG2E, NEG)
    return jnp.concatenate([tiles, jnp.full_like(tiles[:, :1], NEG)], axis=1)


def _attend(q, tiles, stats):
    m, l, acc = stats
    for tile in tiles:
        k, v, bias = tile()
        s = lax.dot_general(q, k, (((1,), (1,)), ((), ())), preferred_element_type=F32)
        parts = [s[:, i:i + LANES] for i in range(0, s.shape[1], LANES)]
        if bias is not None:
            cols = [bias[:, i:i + LANES] for i in range(0, bias.shape[1], LANES)]
            parts = [t + cols[i % len(cols)] for i, t in enumerate(parts)]
        mx = functools.reduce(jnp.maximum, parts)
        m_new = jnp.maximum(m, jnp.max(mx, axis=-1, keepdims=True))
        alpha = jnp.exp2(m - m_new)
        p = [jnp.exp2(t - m_new) for t in parts]
        if l is not None:
            l = alpha * l + functools.reduce(jnp.add, p)
        pv = jnp.dot(jnp.concatenate(p, axis=1).astype(BF16), v, preferred_element_type=F32)
        acc = alpha * acc + pv
        m = m_new
    return m, l, acc


def _fresh_stats(rows, row_sums=True):
    zeros = jnp.zeros((rows, LANES), F32)
    return (jnp.full((rows, LANES), NEG, F32), zeros if row_sums else None, zeros)


def _causal_sweep(q, near, n_far, load_tile, cfar, scratch, side_work=None):
    m_sc, l_sc, acc_sc = scratch
    rows = q.shape[0]
    n_groups = lax.shift_right_logical(n_far, FAR_UNROLL_LOG2)
    rem = n_far - n_groups * FAR_UNROLL
    n_pairs = lax.shift_right_logical(n_groups, 1)

    def save(m, l, acc):
        m_sc[...], acc_sc[...] = m, acc
        if l_sc is not None:
            l_sc[...] = l

    for r in range(FAR_UNROLL):
        @pl.when(rem == r)
        def _():
            extra = [functools.partial(load_tile, n_groups * FAR_UNROLL + i, cfar) for i in range(r)]
            m, l, acc = _attend(q, near + extra, _fresh_stats(rows, l_sc is not None))
            save(m - cfar(), l, acc)
            if side_work is not None:
                side_work()

    def run(first, count):
        tiles = [functools.partial(load_tile, first + u, None) for u in range(count)]
        save(*_attend(q, tiles, (m_sc[...], None if l_sc is None else l_sc[...], acc_sc[...])))

    @pl.when(n_groups > 2 * n_pairs)
    def _():
        run(n_pairs * 2 * FAR_UNROLL, FAR_UNROLL)

    def pair(g, carry):
        run(g * 2 * FAR_UNROLL, 2 * FAR_UNROLL)
        return carry

    lax.fori_loop(0, n_pairs, pair, 0)


def _diff_attn_kernel(q_ref, k_ref, v_ref, bias_ref, cfar_ref, lam_ref, subln_ref, o_ref,
                      m_sc, l_sc, acc_sc, *, tq, tk, lam_init):
    qi = pl.program_id(2)
    q = q_ref[0]
    lane = lax.broadcasted_iota(jnp.int32, q.shape, 1)
    zero = jnp.zeros_like(q)
    q2 = jnp.concatenate([jnp.where(lane < HEAD_DIM, q, zero),
                          jnp.where(lane >= HEAD_DIM, q, zero)], axis=0)

    def load_tile(j, bias):
        return (k_ref[0, pl.ds(j * tk, tk), :], v_ref[0, pl.ds(j * tk, tk), :], bias() if bias else None)

    ratio = tq // tk
    j0 = qi * ratio
    near = [functools.partial(load_tile, j0 + r, lambda r=r: bias_ref[0, r]) for r in range(ratio)]
    near.append(functools.partial(load_tile, jnp.maximum(j0 - 1, 0),
                                  lambda: bias_ref[0, jnp.where(qi >= 1, ratio, ratio + 1)]))
    _causal_sweep(q2, near, jnp.maximum(j0 - 1, 0), load_tile, lambda: cfar_ref[0], (m_sc, l_sc, acc_sc))

    lf = lam_ref[...]
    lam = (jnp.exp(jnp.sum(lf[0:1] * lf[1:2], axis=-1, keepdims=True))
           - jnp.exp(jnp.sum(lf[2:3] * lf[3:4], axis=-1, keepdims=True)) + lam_init)
    on = acc_sc[...] / jnp.sum(l_sc[...], axis=-1, keepdims=True)
    o = on[:tq] - lam * on[tq:]
    ms = jnp.mean(o * o, axis=-1, keepdims=True)
    o = o * lax.rsqrt(ms + NORM_EPS) * subln_ref[...] * (1.0 - lam_init)
    o_ref[0] = o.astype(o_ref.dtype)


def _diff_attention(qkv, rel_bias, lam, subln, layer_idx):
    B, T, D3 = qkv.shape
    D = D3 // 3
    H = DIFF_HEADS
    tq = min(DIFF_TQ, T)
    tk = min(DIFF_TK, tq)
    ratio = tq // tk
    lam_init = 0.8 - 0.6 * math.exp(-0.3 * layer_idx)
    assert 2 * tk - (tk - 1) >= REL_MAX_DIST
    rels = [-r * tk for r in range(ratio)] + [tk]
    nb = len(rels) + 1
    bt = _bias_tiles(rel_bias, rels, tq, tk, None)
    bias = bt.reshape(H, 2, nb, tq, tk).transpose(0, 2, 1, 3, 4).reshape(H, nb, 2 * tq, tk)
    cfar = jnp.repeat(rel_bias.astype(F32)[REL_BUCKETS - 1].reshape(H, 2) * LOG2E, tq, axis=1)
    cfar = jnp.broadcast_to(cfar.reshape(H, 2 * tq, 1), (H, 2 * tq, LANES))
    hb = D // LANES
    return pl.pallas_call(
        functools.partial(_diff_attn_kernel, tq=tq, tk=tk, lam_init=lam_init),
        grid=(B, H, T // tq),
        in_specs=[
            pl.BlockSpec((1, tq, LANES), lambda b, h, i: (b, i, h)),
            pl.BlockSpec((1, T, LANES), lambda b, h, i: (b, 0, hb + h)),
            pl.BlockSpec((1, T, LANES), lambda b, h, i: (b, 0, 2 * hb + h)),
            pl.BlockSpec((1, nb, 2 * tq, tk), lambda b, h, i: (h, 0, 0, 0)),
            pl.BlockSpec((1, 2 * tq, LANES), lambda b, h, i: (h, 0, 0)),
            _const_spec(lam.shape),
            _const_spec((1, LANES)),
        ],
        out_specs=pl.BlockSpec((1, tq, LANES), lambda b, h, i: (b, i, h)),
        out_shape=jax.ShapeDtypeStruct((B, T, D), BF16),
        scratch_shapes=[pltpu.VMEM((2 * tq, LANES), F32)] * 3,
        compiler_params=_cparams("arbitrary", "arbitrary", "arbitrary"),
        name="diff_attention",
    )(qkv, qkv, qkv, bias, cfar, lam.astype(F32), subln.astype(F32).reshape(1, LANES))


def _compress_kernel(a_ref, w1_ref, pos_ref, w2_ref, o_ref):
    half = CMP_STRIDE * HEAD_DIM
    for s in range(2):
        a = a_ref[s, 0, 0]
        w1 = w1_ref[s]
        p = jnp.dot(a, w1[:half], preferred_element_type=F32)
        q = jnp.dot(a, w1[half:], preferred_element_type=F32)
        posb = jnp.dot(jnp.broadcast_to(pos_ref[s], (8, 2 * half)).astype(BF16), w1,
                       preferred_element_type=F32)[0:1]
        hid = p + pltpu.roll(q, q.shape[0] - 1, axis=0) + posb
        hid = 0.5 * hid * (1.0 + jnp.tanh(math.sqrt(2.0 / math.pi) * (hid + 0.044715 * hid * hid * hid)))
        c = jnp.dot(hid.astype(BF16), w2_ref[s], preferred_element_type=F32)
        o_ref[0, 0, s] = jnp.concatenate([c, jnp.zeros_like(c)], axis=1).astype(o_ref.dtype)


def _compress(kv_cmp, cmp_pos, cmp_w1, cmp_w2):
    B, T, _ = kv_cmp.shape
    G, d = NSA_KV_HEADS, HEAD_DIM
    n = T // CMP_STRIDE
    a = kv_cmp.reshape(B, n, CMP_STRIDE, 2, G, d).transpose(3, 0, 4, 1, 2, 5).reshape(2, B, G, n, CMP_STRIDE * d)
    hid = cmp_w1.shape[-1]
    return pl.pallas_call(
        _compress_kernel,
        grid=(B, G),
        in_specs=[
            pl.BlockSpec((2, 1, 1, n, CMP_STRIDE * d), lambda b, g: (0, b, g, 0, 0)),
            _const_spec((2, CMP_LEN * d, hid)),
            _const_spec((2, 1, CMP_LEN * d)),
            _const_spec((2, hid, d)),
        ],
        out_specs=pl.BlockSpec((1, 1, 2, n, LANES), lambda b, g: (b, g, 0, 0, 0)),
        out_shape=jax.ShapeDtypeStruct((B, G, 2, n, LANES), BF16),
        compiler_params=_cparams("arbitrary", "arbitrary"),
        name="nsa_compress",
    )(a, cmp_w1.astype(BF16), cmp_pos.astype(F32).reshape(2, 1, CMP_LEN * d), cmp_w2.astype(BF16))


def _rows_from_lane_blocks(x):
    return jnp.concatenate([x[:, r * LANES:(r + 1) * LANES] for r in range(NSA_GROUP)], axis=0)


def _lane_blocks_from_rows(x):
    tq = x.shape[0] // NSA_GROUP
    return jnp.concatenate([x[r * tq:(r + 1) * tq] for r in range(NSA_GROUP)], axis=1)


def _nsa_cmp_kernel(q_ref, kc_ref, ov_ref, tri_ref, oc_ref, sel_ref, *, tq, top_k):
    oc, sel = _cmp_select(q_ref[0], pl.program_id(2) * tq, kc_ref[0, 0, 0], kc_ref[0, 0, 1],
                          ov_ref[...], tri_ref[...], top_k)
    oc_ref[0] = oc.astype(oc_ref.dtype)
    sel_ref[0, 0] = sel.astype(sel_ref.dtype)


def _cmp_select(q, q0, k_c, v_c, ov, tri, top_k):
    R = NSA_GROUP
    tq = q.shape[0]
    n_cmp = k_c.shape[0]
    q4 = _rows_from_lane_blocks(q)
    s = lax.dot_general(q4, k_c, (((1,), (1,)), ((), ())), preferred_element_type=F32)
    qpos = q0 + lax.broadcasted_iota(jnp.int32, (tq, 1), 0)
    cmp_end = lax.broadcasted_iota(jnp.int32, (1, n_cmp), 1) * CMP_STRIDE + (CMP_LEN - 1)
    valid4 = jnp.concatenate([cmp_end <= qpos] * R, axis=0)
    sm = jnp.where(valid4, s, NEG)
    e = jnp.exp2(sm - jnp.max(sm, axis=-1, keepdims=True))
    any_valid = jnp.concatenate([qpos >= CMP_LEN - 1] * R, axis=0)
    p = e * jnp.where(any_valid, 1.0 / jnp.sum(e, axis=-1, keepdims=True), 0.0)
    oc = _lane_blocks_from_rows(jnp.dot(p.astype(BF16), v_c, preferred_element_type=F32))

    psum = p[0:tq]
    for r in range(1, R):
        psum = psum + p[r * tq:(r + 1) * tq]
    hi = psum.astype(BF16)
    lo = (psum - hi.astype(F32)).astype(BF16)
    imp = jnp.dot(hi, ov, preferred_element_type=F32) + jnp.dot(lo, ov, preferred_element_type=F32)
    j = lax.broadcasted_iota(jnp.int32, (1, LANES), 1)
    qblk = lax.shift_right_logical(qpos, int(math.log2(SLC_LEN)))
    forced = (j == 0) | ((j <= qblk) & (j > qblk - SLC_FORCED_LOCAL))
    imp = jnp.where(forced, FORCE_BONUS, imp)
    imp = jnp.where(j <= qblk, imp, NEG)

    n_slc = tri.shape[0]
    vals = imp.T[:n_slc]
    rem = vals
    cnt = jnp.zeros((1, tq), F32)
    thr = jnp.full((1, tq), NEG, F32)
    for _ in range(top_k):
        mx = jnp.max(rem, axis=0, keepdims=True)
        active = cnt < top_k
        thr = jnp.where(active, mx, thr)
        hit = rem == mx
        cnt = cnt + jnp.where(active, jnp.sum(jnp.where(hit, 1.0, 0.0), axis=0, keepdims=True), 0.0)
        rem = jnp.where(hit, BELOW_NEG, rem)
    above = vals > thr
    n_above = jnp.sum(jnp.where(above, 1.0, 0.0), axis=0, keepdims=True)
    tie = vals == thr
    prefix = jnp.dot(tri, jnp.where(tie, 1.0, 0.0).astype(BF16), preferred_element_type=F32)
    chosen = (above | (tie & (prefix <= top_k - n_above))) & (vals > NEG)
    selb = jnp.where(chosen, 0.0, NEG)
    pieces = [jnp.zeros((HEAD_DIM, tq), F32), selb]
    if HEAD_DIM - n_slc:
        pieces.append(jnp.zeros((HEAD_DIM - n_slc, tq), F32))
    return oc, jnp.concatenate(pieces, axis=0).T


def _nsa_compressed_select(q, kc):
    B, T, _ = q.shape
    G, R = NSA_KV_HEADS, NSA_GROUP
    n_cmp = kc.shape[3]
    n_slc = T // SLC_LEN
    assert n_slc <= HEAD_DIM, "selection mask is carried in the 64 spare contraction lanes"
    top_k = min(SLC_TOPK, n_slc)
    tq = min(CMP_TQ, T)
    nn = np.arange(n_cmp)[:, None]
    jj = np.arange(LANES)[None, :]
    overlap = ((nn * CMP_STRIDE < jj * SLC_LEN + SLC_LEN) & (nn * CMP_STRIDE + CMP_LEN - 1 >= jj * SLC_LEN)
               & (jj < n_slc))
    overlap = overlap & (nn < (T - CMP_LEN) // CMP_STRIDE + 1)
    tri = np.tril(np.ones((n_slc, n_slc), np.float32))
    return pl.pallas_call(
        functools.partial(_nsa_cmp_kernel, tq=tq, top_k=top_k),
        grid=(B, G, T // tq),
        in_specs=[
            pl.BlockSpec((1, tq, R * LANES), lambda b, g, i: (b, i, g)),
            pl.BlockSpec((1, 1, 2, n_cmp, LANES), lambda b, g, i: (b, g, 0, 0, 0)),
            _const_spec((n_cmp, LANES)),
            _const_spec((n_slc, n_slc)),
        ],
        out_specs=[
            pl.BlockSpec((1, tq, R * LANES), lambda b, g, i: (b, i, g)),
            pl.BlockSpec((1, 1, tq, LANES), lambda b, g, i: (b, g, i, 0)),
        ],
        out_shape=[jax.ShapeDtypeStruct((B, T, G * R * LANES), BF16), jax.ShapeDtypeStruct((B, G, T, LANES), BF16)],
        compiler_params=_cparams("arbitrary", "arbitrary", "arbitrary"),
        name="nsa_compressed_select",
    )(q, kc, jnp.asarray(overlap.astype(np.float32), BF16), jnp.asarray(tri, BF16))


def _nsa_main_kernel(q_ref, sel_ref, ka_ref, vs_ref, kw_ref, vw_ref, bias_ref, cfar_ref, g_ref, ge_ref, cp_ref,
                     oc_ref, o_ref, ms_sc, as_sc, aw_sc, *, tq, tk, n_win):
    qi = pl.program_id(2)
    R = NSA_GROUP
    q4w = _rows_from_lane_blocks(q_ref[0])
    q4s = q4w + jnp.concatenate([sel_ref[0, 0]] * R, axis=0)

    def load_s(j, bias):
        return (ka_ref[0, pl.ds(j * tk, tk), :], vs_ref[0, pl.ds(j * tk, tk), :], bias() if bias else None)

    def load_w(j, bias):
        return (kw_ref[0, pl.ds(j * tk, tk), :], vw_ref[0, pl.ds(j * tk, tk), :], bias() if bias else None)

    near_s, near_w = [], []
    for t in range(n_win):
        j = jnp.maximum(qi - t, 0)
        b = (lambda: bias_ref[0, 0]) if t == 0 else (lambda t=t: bias_ref[0, jnp.where(qi >= t, t, n_win)])
        near_w.append(functools.partial(load_w, j, b))
        if t <= 1:
            near_s.append(functools.partial(load_s, j, b))

    def window_branch():
        _, _, aw_sc[...] = _attend(q4w, near_w, _fresh_stats(R * tq, row_sums=False))

    _causal_sweep(q4s, near_s, jnp.maximum(qi - 1, 0), load_s, lambda: cfar_ref[0], (ms_sc, None, as_sc),
                  side_work=window_branch)

    def normalised(acc):
        return acc / jnp.max(acc[:, HEAD_DIM:], axis=-1, keepdims=True)

    o_s = _lane_blocks_from_rows(normalised(as_sc[...]))
    o_w = _lane_blocks_from_rows(normalised(aw_sc[...]))
    g = g_ref[0]
    lane = lax.broadcasted_iota(jnp.int32, g.shape, 1)
    hi = g.astype(BF16)
    r1 = g - hi.astype(F32)
    mid = r1.astype(BF16)
    lo = (r1 - mid.astype(F32)).astype(BF16)
    n_g = 3 * R
    g_split = jnp.where(lane < n_g, hi, jnp.where(lane < 2 * n_g, mid, lo))
    ge = jnp.dot(g_split, ge_ref[...], preferred_element_type=F32)
    w = R * LANES
    o = ge[:, :w] * oc_ref[0].astype(F32) + ge[:, w:2 * w] * o_s + ge[:, 2 * w:] * o_w
    o_ref[0] = jnp.dot(o.astype(BF16), cp_ref[...], preferred_element_type=F32).astype(o_ref.dtype)


def _nsa_main(q, sel, kv, gates, o_cmp, rel_bias):
    B, T, _ = q.shape
    G, R = NSA_KV_HEADS, NSA_GROUP
    D = G * R * HEAD_DIM
    tq = min(NSA_TQ, T)
    tk = tq
    n_win = (WINDOW - 2) // tk + 2
    rels = [t * tk for t in range(n_win)]
    assert tk + tq - 1 < WINDOW and 2 * tk - (tk - 1) >= REL_MAX_DIST
    bt = _bias_tiles(rel_bias, rels, tq, tk, WINDOW)
    nb = n_win + 1
    bias = bt.reshape(G, R, nb, tq, tk).transpose(0, 2, 1, 3, 4).reshape(G, nb, R * tq, tk)
    cfar = jnp.repeat(rel_bias.astype(F32)[REL_BUCKETS - 1].reshape(G, R) * LOG2E, tq, axis=1)
    cfar = jnp.broadcast_to(cfar.reshape(G, R * tq, 1), (G, R * tq, LANES))
    ge = np.zeros((LANES, 3 * R * LANES), np.float32)
    for rep in range(3):
        for r in range(R):
            for br in range(3):
                c0 = br * R * LANES + r * LANES
                ge[rep * 3 * R + 3 * r + br, c0:c0 + HEAD_DIM] = 1.0
    cp = np.zeros((R * LANES, R * HEAD_DIM), np.float32)
    for r in range(R):
        cp[r * LANES + np.arange(HEAD_DIM), r * HEAD_DIM + np.arange(HEAD_DIM)] = 1.0
    blk = lambda k: pl.BlockSpec((1, T, LANES), lambda b, g, i: (b, 0, 4 * g + k))
    return pl.pallas_call(
        functools.partial(_nsa_main_kernel, tq=tq, tk=tk, n_win=n_win),
        grid=(B, G, T // tq),
        in_specs=[
            pl.BlockSpec((1, tq, R * LANES), lambda b, g, i: (b, i, g)),
            pl.BlockSpec((1, 1, tq, LANES), lambda b, g, i: (b, g, i, 0)),
            blk(0), blk(1), blk(2), blk(3),
            pl.BlockSpec((1, nb, R * tq, tk), lambda b, g, i: (g, 0, 0, 0)),
            pl.BlockSpec((1, R * tq, LANES), lambda b, g, i: (g, 0, 0)),
            pl.BlockSpec((1, tq, LANES), lambda b, g, i: (b, i, g)),
            _const_spec(ge.shape),
            _const_spec(cp.shape),
            pl.BlockSpec((1, tq, R * LANES), lambda b, g, i: (b, i, g)),
        ],
        out_specs=pl.BlockSpec((1, tq, R * HEAD_DIM), lambda b, g, i: (b, i, g)),
        out_shape=jax.ShapeDtypeStruct((B, T, D), BF16),
        scratch_shapes=[pltpu.VMEM((R * tq, LANES), F32)] * 3,
        compiler_params=_cparams("arbitrary", "arbitrary", "arbitrary"),
        name="nsa_main",
    )(q, sel, kv, kv, kv, kv, bias, cfar, gates, jnp.asarray(ge, BF16), jnp.asarray(cp, BF16), o_cmp)


def kernel(x, c, rel_bias, ada_w, ada_b, attn_norm, mlp_norm, mlp_w1, mlp_w2, a_w_in, a_w_out, a_lambda, a_subln, kv_ada_w, kv_ada_b, kv_norm, w_kv, cmp_pos, cmp_w1, cmp_w2, b_w_in, b_w_out, final_norm):
    B, T, D = x.shape
    depth = ada_w.shape[0]
    n_a = a_w_in.shape[0]
    G, R, d = NSA_KV_HEADS, NSA_GROUP, HEAD_DIM

    mod = _adaln(c, ada_w, ada_b)
    kv_mod = _adaln(c, kv_ada_w[None], kv_ada_b[None])[0]

    wkv = w_kv.reshape(D, 6, G, d)
    z = jnp.zeros((D, G, d), w_kv.dtype)
    w_cmp = wkv[:, 0:2].reshape(D, 2 * G * d).astype(BF16)
    w_sw = jnp.stack([wkv[:, 2], z, wkv[:, 3], z, wkv[:, 4], wkv[:, 5], wkv[:, 5], z], axis=2)
    w_sw = w_sw.reshape(D, G * 4 * LANES).astype(BF16)
    kv_plan = [(0, c0, w, 0, c0, "plain") for c0, w in _chunks(2 * G * d)]
    kv_plan += [(1, c0, w, 1, c0, "kv_group") for c0, w in _chunks(G * 4 * LANES, 4 * LANES)]

    shared = None
    for layer in range(depth):
        sh_a, sc_a, gt_a, sh_m, sc_m, gt_m = jnp.split(mod[layer], 6, axis=-1)
        if layer < n_a:
            w_in = a_w_in[layer].astype(BF16)
            plan = [(0, c0, w, 0, c0, "scale" if c0 < D else "plain") for c0, w in _chunks(3 * D)]
            (qkv,) = _proj(x, attn_norm[layer], sh_a, sc_a, [w_in], [(3 * D, BF16)], plan, "diff_qkv_proj")
            mix = _diff_attention(qkv, rel_bias, a_lambda[layer], a_subln[layer], layer)
            w_out = a_w_out[layer].astype(BF16)
        else:
            i = layer - n_a
            w_q = b_w_in[i][:, :D].astype(BF16)
            w_g = jnp.tile(b_w_in[i][:, D:].reshape(D, G, 3 * R), (1, 1, 3))
            w_g = jnp.pad(w_g, ((0, 0), (0, 0), (0, LANES - 9 * R))).reshape(D, G * LANES).astype(BF16)
            plan = [(0, c0, w, 0, 2 * c0, "scale_pad") for c0, w in _chunks(D)]
            plan += [(1, 0, G * LANES, 1, 0, "sigmoid")]
            q, gates = _proj(x, attn_norm[layer], sh_a, sc_a, [w_q, w_g], [(2 * D, BF16), (G * LANES, F32)], plan,
                             "nsa_in_proj")
            kc, kv_sw = shared
            o_cmp, sel = _nsa_compressed_select(q, kc)
            mix = _nsa_main(q, sel, kv_sw, gates, o_cmp, rel_bias)
            w_out = b_w_out[i].astype(BF16)
        x = _mix_mlp_residual(x, mix, w_out, gt_a, mlp_norm[layer], sh_m, sc_m, gt_m, mlp_w1[layer].astype(BF16),
                              mlp_w2[layer].astype(BF16), final_norm, final=(layer == depth - 1))
        if layer == n_a - 1:
            sh_kv, sc_kv = jnp.split(kv_mod, 2, axis=-1)
            kv_cmp, kv_sw = _proj(x, kv_norm, sh_kv, sc_kv, [w_cmp, w_sw], [(2 * G * d, BF16), (G * 4 * LANES, BF16)],
                                  kv_plan, "nsa_kv_proj")
            shared = (_compress(kv_cmp, cmp_pos, cmp_w1, cmp_w2), kv_sw)
    return x
```

```python
import functools
import math

import numpy as np
import jax
import jax.numpy as jnp
from jax import lax
from jax.experimental import pallas as pl
from jax.experimental.pallas import tpu as pltpu

F32 = jnp.float32
BF16 = jnp.bfloat16

NEG = -1e30
BELOW_NEG = -3e38
NORM_EPS = 1e-6

HEAD_DIM = 64
DIFF_HEADS = 8
NSA_KV_HEADS = 4
NSA_GROUP = 4
REL_BUCKETS = 32
REL_MAX_DIST = 128
CMP_LEN = 32
CMP_STRIDE = 16
SLC_LEN = 64
SLC_TOPK = 16
SLC_FORCED_LOCAL = 2
FORCE_BONUS = 1e4
WINDOW = 512

LANES = 128
VMEM_LIMIT_BYTES = 56 * 1024 * 1024

PROJ_ROWS = 512
DIFF_TQ = 512
DIFF_TK = 256
FAR_UNROLL_LOG2 = 3
FAR_UNROLL = 1 << FAR_UNROLL_LOG2
LOG2E = math.log2(math.e)
NSA_TQ = 256
CMP_TQ = 512
FF_CHUNK = 1024


def _cparams(*sem):
    return pltpu.CompilerParams(dimension_semantics=sem, vmem_limit_bytes=VMEM_LIMIT_BYTES)


def _const_spec(shape):
    nd = len(shape)
    return pl.BlockSpec(shape, lambda *_: (0,) * nd, pipeline_mode=pl.Buffered(1))


def _adaln_kernel(c_ref, w_ref, b_ref, o_ref):
    c = c_ref[...]
    c_act = (c * (1.0 / (1.0 + jnp.exp(-c)))).astype(BF16)
    o_ref[0] = jnp.dot(c_act, w_ref[0].astype(BF16), preferred_element_type=F32) + b_ref[0]


def _adaln(c, w, b, tn=2048):
    L, D, N = w.shape
    B = c.shape[0]
    tn = min(tn, N)
    return pl.pallas_call(
        _adaln_kernel,
        grid=(L, N // tn),
        in_specs=[
            pl.BlockSpec((B, D), lambda l, j: (0, 0)),
            pl.BlockSpec((1, D, tn), lambda l, j: (l, 0, j)),
            pl.BlockSpec((1, 1, tn), lambda l, j: (l, 0, j)),
        ],
        out_specs=pl.BlockSpec((1, B, tn), lambda l, j: (l, 0, j)),
        out_shape=jax.ShapeDtypeStruct((L, B, N), F32),
        compiler_params=_cparams("arbitrary", "arbitrary"),
        name="adaln",
    )(c, w, b.reshape(L, 1, N))


def _rms_mod(x, g, shift, scale):
    ms = jnp.mean(x * x, axis=-1, keepdims=True)
    y = x * lax.rsqrt(ms + NORM_EPS) * g
    return y * (1.0 + scale) + shift


def _proj_kernel(x_ref, g_ref, sh_ref, sc_ref, *refs, n_w, plan, rows):
    w_refs, o_refs = refs[:n_w], refs[n_w:]
    h = _rms_mod(x_ref[0], g_ref[...], sh_ref[0], sc_ref[0]).astype(BF16)
    for (wi, c0, width, oi, o0, mode) in plan:
        y = jnp.dot(h, w_refs[wi][:, c0:c0 + width], preferred_element_type=F32)
        if mode in ("scale", "scale_pad"):
            y = y * (HEAD_DIM ** -0.5 * LOG2E)
        elif mode == "sigmoid":
            y = 1.0 / (1.0 + jnp.exp(-y))
        elif mode == "kv_group":
            pos = pl.program_id(1) * rows + lax.broadcasted_iota(jnp.int32, y.shape, 0)
            col = lax.broadcasted_iota(jnp.int32, y.shape, 1)
            blk = col // LANES
            upper = col % LANES - HEAD_DIM
            hot = (blk == 0) & (upper == lax.shift_right_logical(pos, int(math.log2(SLC_LEN))))
            ones = ((blk == 1) | (blk == 3)) & (upper >= 0)
            y = y + jnp.where(hot | ones, 1.0, 0.0)
        y = y.astype(o_refs[oi].dtype)
        if mode == "scale_pad":
            zeros = jnp.zeros((y.shape[0], HEAD_DIM), y.dtype)
            pieces = []
            for c in range(0, width, HEAD_DIM):
                pieces += [y[:, c:c + HEAD_DIM], zeros]
            y = jnp.concatenate(pieces, axis=1)
        o_refs[oi][0, :, o0:o0 + y.shape[1]] = y


def _proj(x, g, shift, scale, weights, outs, plan, name):
    B, T, D = x.shape
    rows = min(PROJ_ROWS, T)
    in_specs = [
        pl.BlockSpec((1, rows, D), lambda b, i: (b, i, 0)),
        _const_spec((1, D)),
        pl.BlockSpec((1, 1, D), lambda b, i: (b, 0, 0)),
        pl.BlockSpec((1, 1, D), lambda b, i: (b, 0, 0)),
    ] + [_const_spec(w.shape) for w in weights]
    out_specs = [pl.BlockSpec((1, rows, n), lambda b, i: (b, i, 0)) for n, _ in outs]
    out_shape = [jax.ShapeDtypeStruct((B, T, n), dt) for n, dt in outs]
    res = pl.pallas_call(
        functools.partial(_proj_kernel, n_w=len(weights), plan=tuple(plan), rows=rows),
        grid=(B, T // rows),
        in_specs=in_specs,
        out_specs=out_specs,
        out_shape=out_shape,
        compiler_params=_cparams("arbitrary", "arbitrary"),
        name=name,
    )(x, g.reshape(1, D), shift.reshape(B, 1, D), scale.reshape(B, 1, D), *weights)
    return res


def _chunks(n, step=512):
    return [(c, min(step, n - c)) for c in range(0, n, step)]


def _mix_mlp_kernel(x_ref, o_ref, wo_ref, ga_ref, g_ref, sh_ref, sc_ref, gm_ref, w1_ref, w2_ref, fg_ref, y_ref,
                    *, d_ff, final):
    x = x_ref[0] + ga_ref[0] * jnp.dot(o_ref[0], wo_ref[...], preferred_element_type=F32)
    h = _rms_mod(x, g_ref[...], sh_ref[0], sc_ref[0]).astype(BF16)
    acc = jnp.zeros(x.shape, F32)
    for c0 in range(0, d_ff, FF_CHUNK):
        a = jnp.dot(h, w1_ref[:, c0:c0 + FF_CHUNK], preferred_element_type=F32)
        a = jnp.square(jnp.maximum(a, 0.0)).astype(BF16)
        acc = acc + jnp.dot(a, w2_ref[c0:c0 + FF_CHUNK, :], preferred_element_type=F32)
    y = x + gm_ref[0] * acc
    if final:
        ms = jnp.mean(y * y, axis=-1, keepdims=True)
        y = y * lax.rsqrt(ms + NORM_EPS) * fg_ref[...]
    y_ref[0] = y


def _mix_mlp_residual(x, o, w_out, gate_a, g, shift, scale, gate_m, w1, w2, final_gain, final):
    B, T, D = x.shape
    d_ff = w1.shape[1]
    rows = min(PROJ_ROWS, T)
    tile = pl.BlockSpec((1, rows, D), lambda b, i: (b, i, 0))
    vec = pl.BlockSpec((1, 1, D), lambda b, i: (b, 0, 0))
    per_batch = lambda v: v.reshape(B, 1, D)
    return pl.pallas_call(
        functools.partial(_mix_mlp_kernel, d_ff=d_ff, final=final),
        grid=(B, T // rows),
        in_specs=[tile, tile, _const_spec(w_out.shape), vec,
                  _const_spec((1, D)), vec, vec, vec,
                  _const_spec(w1.shape), _const_spec(w2.shape), _const_spec((1, D))],
        out_specs=tile,
        out_shape=jax.ShapeDtypeStruct((B, T, D), F32),
        compiler_params=_cparams("arbitrary", "arbitrary"),
        name="mix_mlp_residual",
    )(x, o, w_out, per_batch(gate_a), g.reshape(1, D), per_batch(shift), per_batch(scale), per_batch(gate_m),
      w1, w2, final_gain.reshape(1, D))


def _t5_bucket(dist):
    n = jnp.maximum(dist, 0)
    max_exact = REL_BUCKETS // 2
    nf = jnp.maximum(n, 1).astype(F32)
    large = max_exact + (jnp.log(nf / max_exact) / math.log(REL_MAX_DIST / max_exact)
                         * (REL_BUCKETS - max_exact)).astype(jnp.int32)
    large = jnp.minimum(large, REL_BUCKETS - 1)
    return jnp.where(n < max_exact, n, large)


def _bias_tiles(rel_bias, rels, tq, tk, window):
    iq = np.arange(tq)[:, None]
    ik = np.arange(tk)[None, :]
    dist = jnp.asarray(np.stack([r + iq - ik for r in rels]).astype(np.int32))
    hot = (_t5_bucket(dist)[..., None] == jnp.arange(REL_BUCKETS)).astype(F32)
    b = jnp.einsum("ntkb,bh->hntk", hot, rel_bias.astype(F32), precision=lax.Precision.HIGHEST)
    ok = dist >= 0
    if window is not None:
        ok = ok & (dist < window)
    tiles = jnp.where(ok[None], b * LOG2E, NEG)
    return jnp.concatenate([tiles, jnp.full_like(tiles[:, :1], NEG)], axis=1)


def _attend(q, tiles, stats):
    m, l, acc = stats
    for tile in tiles:
        k, v, bias = tile()
        s = lax.dot_general(q, k, (((1,), (1,)), ((), ())), preferred_element_type=F32)
        parts = [s[:, i:i + LANES] for i in range(0, s.shape[1], LANES)]
        if bias is not None:
            cols = [bias[:, i:i + LANES] for i in range(0, bias.shape[1], LANES)]
            parts = [t + cols[i % len(cols)] for i, t in enumerate(parts)]
        mx = functools.reduce(jnp.maximum, parts)
        m_new = jnp.maximum(m, jnp.max(mx, axis=-1, keepdims=True))
        alpha = jnp.exp2(m - m_new)
        p = [jnp.exp2(t - m_new) for t in parts]
        if l is not None:
            l = alpha * l + functools.reduce(jnp.add, p)
        pv = jnp.dot(jnp.concatenate(p, axis=1).astype(BF16), v, preferred_element_type=F32)
        acc = alpha * acc + pv
        m = m_new
    return m, l, acc


def _fresh_stats(rows, row_sums=True):
    zeros = jnp.zeros((rows, LANES), F32)
    return (jnp.full((rows, LANES), NEG, F32), zeros if row_sums else None, zeros)


def _causal_sweep(q, near, n_far, load_tile, cfar, scratch, side_work=None):
    m_sc, l_sc, acc_sc = scratch
    rows = q.shape[0]
    n_groups = lax.shift_right_logical(n_far, FAR_UNROLL_LOG2)
    rem = n_far - n_groups * FAR_UNROLL

    def save(m, l, acc):
        m_sc[...], acc_sc[...] = m, acc
        if l_sc is not None:
            l_sc[...] = l

    for r in range(FAR_UNROLL):
        @pl.when(rem == r)
        def _():
            extra = [functools.partial(load_tile, n_groups * FAR_UNROLL + i, cfar) for i in range(r)]
            m, l, acc = _attend(q, near + extra, _fresh_stats(rows, l_sc is not None))
            save(m - cfar(), l, acc)
            if side_work is not None:
                side_work()

    def group(g, carry):
        tiles = [functools.partial(load_tile, g * FAR_UNROLL + u, None) for u in range(FAR_UNROLL)]
        save(*_attend(q, tiles, (m_sc[...], None if l_sc is None else l_sc[...], acc_sc[...])))
        return carry

    lax.fori_loop(0, n_groups, group, 0)


def _diff_attn_kernel(q_ref, k_ref, v_ref, bias_ref, cfar_ref, lam_ref, subln_ref, o_ref,
                      m_sc, l_sc, acc_sc, *, tq, tk, lam_init):
    qi = pl.program_id(2)
    q = q_ref[0]
    lane = lax.broadcasted_iota(jnp.int32, q.shape, 1)
    zero = jnp.zeros_like(q)
    q2 = jnp.concatenate([jnp.where(lane < HEAD_DIM, q, zero),
                          jnp.where(lane >= HEAD_DIM, q, zero)], axis=0)

    def load_tile(j, bias):
        return (k_ref[0, pl.ds(j * tk, tk), :], v_ref[0, pl.ds(j * tk, tk), :], bias() if bias else None)

    ratio = tq // tk
    j0 = qi * ratio
    near = [functools.partial(load_tile, j0 + r, lambda r=r: bias_ref[0, r]) for r in range(ratio)]
    near.append(functools.partial(load_tile, jnp.maximum(j0 - 1, 0),
                                  lambda: bias_ref[0, jnp.where(qi >= 1, ratio, ratio + 1)]))
    _causal_sweep(q2, near, jnp.maximum(j0 - 1, 0), load_tile, lambda: cfar_ref[0], (m_sc, l_sc, acc_sc))

    lf = lam_ref[...]
    lam = (jnp.exp(jnp.sum(lf[0:1] * lf[1:2], axis=-1, keepdims=True))
           - jnp.exp(jnp.sum(lf[2:3] * lf[3:4], axis=-1, keepdims=True)) + lam_init)
    on = acc_sc[...] / jnp.sum(l_sc[...], axis=-1, keepdims=True)
    o = on[:tq] - lam * on[tq:]
    ms = jnp.mean(o * o, axis=-1, keepdims=True)
    o = o * lax.rsqrt(ms + NORM_EPS) * subln_ref[...] * (1.0 - lam_init)
    o_ref[0] = o.astype(o_ref.dtype)


def _diff_attention(qkv, rel_bias, lam, subln, layer_idx):
    B, T, D3 = qkv.shape
    D = D3 // 3
    H = DIFF_HEADS
    tq = min(DIFF_TQ, T)
    tk = min(DIFF_TK, tq)
    ratio = tq // tk
    lam_init = 0.8 - 0.6 * math.exp(-0.3 * layer_idx)
    assert 2 * tk - (tk - 1) >= REL_MAX_DIST
    rels = [-r * tk for r in range(ratio)] + [tk]
    nb = len(rels) + 1
    bt = _bias_tiles(rel_bias, rels, tq, tk, None)
    bias = bt.reshape(H, 2, nb, tq, tk).transpose(0, 2, 1, 3, 4).reshape(H, nb, 2 * tq, tk)
    cfar = jnp.repeat(rel_bias.astype(F32)[REL_BUCKETS - 1].reshape(H, 2) * LOG2E, tq, axis=1)
    cfar = jnp.broadcast_to(cfar.reshape(H, 2 * tq, 1), (H, 2 * tq, LANES))
    hb = D // LANES
    return pl.pallas_call(
        functools.partial(_diff_attn_kernel, tq=tq, tk=tk, lam_init=lam_init),
        grid=(B, H, T // tq),
        in_specs=[
            pl.BlockSpec((1, tq, LANES), lambda b, h, i: (b, i, h)),
            pl.BlockSpec((1, T, LANES), lambda b, h, i: (b, 0, hb + h)),
            pl.BlockSpec((1, T, LANES), lambda b, h, i: (b, 0, 2 * hb + h)),
            pl.BlockSpec((1, nb, 2 * tq, tk), lambda b, h, i: (h, 0, 0, 0)),
            pl.BlockSpec((1, 2 * tq, LANES), lambda b, h, i: (h, 0, 0)),
            _const_spec(lam.shape),
            _const_spec((1, LANES)),
        ],
        out_specs=pl.BlockSpec((1, tq, LANES), lambda b, h, i: (b, i, h)),
        out_shape=jax.ShapeDtypeStruct((B, T, D), BF16),
        scratch_shapes=[pltpu.VMEM((2 * tq, LANES), F32)] * 3,
        compiler_params=_cparams("arbitrary", "arbitrary", "arbitrary"),
        name="diff_attention",
    )(qkv, qkv, qkv, bias, cfar, lam.astype(F32), subln.astype(F32).reshape(1, LANES))


def _compress_kernel(a_ref, w1_ref, pos_ref, w2_ref, o_ref):
    half = CMP_STRIDE * HEAD_DIM
    for s in range(2):
        a = a_ref[s, 0, 0]
        w1 = w1_ref[s]
        p = jnp.dot(a, w1[:half], preferred_element_type=F32)
        q = jnp.dot(a, w1[half:], preferred_element_type=F32)
        posb = jnp.dot(jnp.broadcast_to(pos_ref[s], (8, 2 * half)).astype(BF16), w1,
                       preferred_element_type=F32)[0:1]
        hid = p + pltpu.roll(q, q.shape[0] - 1, axis=0) + posb
        hid = 0.5 * hid * (1.0 + jnp.tanh(math.sqrt(2.0 / math.pi) * (hid + 0.044715 * hid * hid * hid)))
        c = jnp.dot(hid.astype(BF16), w2_ref[s], preferred_element_type=F32)
        o_ref[0, 0, s] = jnp.concatenate([c, jnp.zeros_like(c)], axis=1).astype(o_ref.dtype)


def _compress(kv_cmp, cmp_pos, cmp_w1, cmp_w2):
    B, T, _ = kv_cmp.shape
    G, d = NSA_KV_HEADS, HEAD_DIM
    n = T // CMP_STRIDE
    a = kv_cmp.reshape(B, n, CMP_STRIDE, 2, G, d).transpose(3, 0, 4, 1, 2, 5).reshape(2, B, G, n, CMP_STRIDE * d)
    hid = cmp_w1.shape[-1]
    return pl.pallas_call(
        _compress_kernel,
        grid=(B, G),
        in_specs=[
            pl.BlockSpec((2, 1, 1, n, CMP_STRIDE * d), lambda b, g: (0, b, g, 0, 0)),
            _const_spec((2, CMP_LEN * d, hid)),
            _const_spec((2, 1, CMP_LEN * d)),
            _const_spec((2, hid, d)),
        ],
        out_specs=pl.BlockSpec((1, 1, 2, n, LANES), lambda b, g: (b, g, 0, 0, 0)),
        out_shape=jax.ShapeDtypeStruct((B, G, 2, n, LANES), BF16),
        compiler_params=_cparams("arbitrary", "arbitrary"),
        name="nsa_compress",
    )(a, cmp_w1.astype(BF16), cmp_pos.astype(F32).reshape(2, 1, CMP_LEN * d), cmp_w2.astype(BF16))


def _rows_from_lane_blocks(x):
    return jnp.concatenate([x[:, r * LANES:(r + 1) * LANES] for r in range(NSA_GROUP)], axis=0)


def _lane_blocks_from_rows(x):
    tq = x.shape[0] // NSA_GROUP
    return jnp.concatenate([x[r * tq:(r + 1) * tq] for r in range(NSA_GROUP)], axis=1)


def _nsa_cmp_kernel(q_ref, kc_ref, ov_ref, tri_ref, oc_ref, sel_ref, *, tq, top_k):
    oc, sel = _cmp_select(q_ref[0], pl.program_id(2) * tq, kc_ref[0, 0, 0], kc_ref[0, 0, 1],
                          ov_ref[...], tri_ref[...], top_k)
    oc_ref[0] = oc.astype(oc_ref.dtype)
    sel_ref[0, 0] = sel.astype(sel_ref.dtype)


def _cmp_select(q, q0, k_c, v_c, ov, tri, top_k):
    R = NSA_GROUP
    tq = q.shape[0]
    n_cmp = k_c.shape[0]
    q4 = _rows_from_lane_blocks(q)
    s = lax.dot_general(q4, k_c, (((1,), (1,)), ((), ())), preferred_element_type=F32)
    qpos = q0 + lax.broadcasted_iota(jnp.int32, (tq, 1), 0)
    cmp_end = lax.broadcasted_iota(jnp.int32, (1, n_cmp), 1) * CMP_STRIDE + (CMP_LEN - 1)
    valid4 = jnp.concatenate([cmp_end <= qpos] * R, axis=0)
    sm = jnp.where(valid4, s, NEG)
    e = jnp.exp2(sm - jnp.max(sm, axis=-1, keepdims=True))
    any_valid = jnp.concatenate([qpos >= CMP_LEN - 1] * R, axis=0)
    p = e * jnp.where(any_valid, 1.0 / jnp.sum(e, axis=-1, keepdims=True), 0.0)
    oc = _lane_blocks_from_rows(jnp.dot(p.astype(BF16), v_c, preferred_element_type=F32))

    psum = p[0:tq]
    for r in range(1, R):
        psum = psum + p[r * tq:(r + 1) * tq]
    hi = psum.astype(BF16)
    lo = (psum - hi.astype(F32)).astype(BF16)
    imp = jnp.dot(hi, ov, preferred_element_type=F32) + jnp.dot(lo, ov, preferred_element_type=F32)
    j = lax.broadcasted_iota(jnp.int32, (1, LANES), 1)
    qblk = lax.shift_right_logical(qpos, int(math.log2(SLC_LEN)))
    forced = (j == 0) | ((j <= qblk) & (j > qblk - SLC_FORCED_LOCAL))
    imp = jnp.where(forced, FORCE_BONUS, imp)
    imp = jnp.where(j <= qblk, imp, NEG)

    n_slc = tri.shape[0]
    vals = imp.T[:n_slc]
    rem = vals
    cnt = jnp.zeros((1, tq), F32)
    thr = jnp.full((1, tq), NEG, F32)
    for _ in range(top_k):
        mx = jnp.max(rem, axis=0, keepdims=True)
        active = cnt < top_k
        thr = jnp.where(active, mx, thr)
        hit = rem == mx
        cnt = cnt + jnp.where(active, jnp.sum(jnp.where(hit, 1.0, 0.0), axis=0, keepdims=True), 0.0)
        rem = jnp.where(hit, BELOW_NEG, rem)
    above = vals > thr
    n_above = jnp.sum(jnp.where(above, 1.0, 0.0), axis=0, keepdims=True)
    tie = vals == thr
    prefix = jnp.dot(tri, jnp.where(tie, 1.0, 0.0).astype(BF16), preferred_element_type=F32)
    chosen = (above | (tie & (prefix <= top_k - n_above))) & (vals > NEG)
    selb = jnp.where(chosen, 0.0, NEG)
    pieces = [jnp.zeros((HEAD_DIM, tq), F32), selb]
    if HEAD_DIM - n_slc:
        pieces.append(jnp.zeros((HEAD_DIM - n_slc, tq), F32))
    return oc, jnp.concatenate(pieces, axis=0).T


def _nsa_compressed_select(q, kc):
    B, T, _ = q.shape
    G, R = NSA_KV_HEADS, NSA_GROUP
    n_cmp = kc.shape[3]
    n_slc = T // SLC_LEN
    assert n_slc <= HEAD_DIM, "selection mask is carried in the 64 spare contraction lanes"
    top_k = min(SLC_TOPK, n_slc)
    tq = min(CMP_TQ, T)
    nn = np.arange(n_cmp)[:, None]
    jj = np.arange(LANES)[None, :]
    overlap = ((nn * CMP_STRIDE < jj * SLC_LEN + SLC_LEN) & (nn * CMP_STRIDE + CMP_LEN - 1 >= jj * SLC_LEN)
               & (jj < n_slc))
    overlap = overlap & (nn < (T - CMP_LEN) // CMP_STRIDE + 1)
    tri = np.tril(np.ones((n_slc, n_slc), np.float32))
    return pl.pallas_call(
        functools.partial(_nsa_cmp_kernel, tq=tq, top_k=top_k),
        grid=(B, G, T // tq),
        in_specs=[
            pl.BlockSpec((1, tq, R * LANES), lambda b, g, i: (b, i, g)),
            pl.BlockSpec((1, 1, 2, n_cmp, LANES), lambda b, g, i: (b, g, 0, 0, 0)),
            _const_spec((n_cmp, LANES)),
            _const_spec((n_slc, n_slc)),
        ],
        out_specs=[
            pl.BlockSpec((1, tq, R * LANES), lambda b, g, i: (b, i, g)),
            pl.BlockSpec((1, 1, tq, LANES), lambda b, g, i: (b, g, i, 0)),
        ],
        out_shape=[jax.ShapeDtypeStruct((B, T, G * R * LANES), BF16), jax.ShapeDtypeStruct((B, G, T, LANES), BF16)],
        compiler_params=_cparams("arbitrary", "arbitrary", "arbitrary"),
        name="nsa_compressed_select",
    )(q, kc, jnp.asarray(overlap.astype(np.float32), BF16), jnp.asarray(tri, BF16))


def _nsa_main_kernel(q_ref, sel_ref, ka_ref, vs_ref, kw_ref, vw_ref, bias_ref, cfar_ref, g_ref, ge_ref, cp_ref,
                     oc_ref, o_ref, ms_sc, as_sc, aw_sc, *, tq, tk, n_win):
    qi = pl.program_id(2)
    R = NSA_GROUP
    q4w = _rows_from_lane_blocks(q_ref[0])
    q4s = q4w + jnp.concatenate([sel_ref[0, 0]] * R, axis=0)

    def load_s(j, bias):
        return (ka_ref[0, pl.ds(j * tk, tk), :], vs_ref[0, pl.ds(j * tk, tk), :], bias() if bias else None)

    def load_w(j, bias):
        return (kw_ref[0, pl.ds(j * tk, tk), :], vw_ref[0, pl.ds(j * tk, tk), :], bias() if bias else None)

    near_s, near_w = [], []
    for t in range(n_win):
        j = jnp.maximum(qi - t, 0)
        b = (lambda: bias_ref[0, 0]) if t == 0 else (lambda t=t: bias_ref[0, jnp.where(qi >= t, t, n_win)])
        near_w.append(functools.partial(load_w, j, b))
        if t <= 1:
            near_s.append(functools.partial(load_s, j, b))

    def window_branch():
        _, _, aw_sc[...] = _attend(q4w, near_w, _fresh_stats(R * tq, row_sums=False))

    _causal_sweep(q4s, near_s, jnp.maximum(qi - 1, 0), load_s, lambda: cfar_ref[0], (ms_sc, None, as_sc),
                  side_work=window_branch)

    def normalised(acc):
        return acc / jnp.max(acc[:, HEAD_DIM:], axis=-1, keepdims=True)

    o_s = _lane_blocks_from_rows(normalised(as_sc[...]))
    o_w = _lane_blocks_from_rows(normalised(aw_sc[...]))
    g = g_ref[0]
    lane = lax.broadcasted_iota(jnp.int32, g.shape, 1)
    hi = g.astype(BF16)
    r1 = g - hi.astype(F32)
    mid = r1.astype(BF16)
    lo = (r1 - mid.astype(F32)).astype(BF16)
    n_g = 3 * R
    g_split = jnp.where(lane < n_g, hi, jnp.where(lane < 2 * n_g, mid, lo))
    ge = jnp.dot(g_split, ge_ref[...], preferred_element_type=F32)
    w = R * LANES
    o = ge[:, :w] * oc_ref[0].astype(F32) + ge[:, w:2 * w] * o_s + ge[:, 2 * w:] * o_w
    o_ref[0] = jnp.dot(o.astype(BF16), cp_ref[...], preferred_element_type=F32).astype(o_ref.dtype)


def _nsa_main(q, sel, kv, gates, o_cmp, rel_bias):
    B, T, _ = q.shape
    G, R = NSA_KV_HEADS, NSA_GROUP
    D = G * R * HEAD_DIM
    tq = min(NSA_TQ, T)
    tk = tq
    n_win = (WINDOW - 2) // tk + 2
    rels = [t * tk for t in range(n_win)]
    assert tk + tq - 1 < WINDOW and 2 * tk - (tk - 1) >= REL_MAX_DIST
    bt = _bias_tiles(rel_bias, rels, tq, tk, WINDOW)
    nb = n_win + 1
    bias = bt.reshape(G, R, nb, tq, tk).transpose(0, 2, 1, 3, 4).reshape(G, nb, R * tq, tk)
    cfar = jnp.repeat(rel_bias.astype(F32)[REL_BUCKETS - 1].reshape(G, R) * LOG2E, tq, axis=1)
    cfar = jnp.broadcast_to(cfar.reshape(G, R * tq, 1), (G, R * tq, LANES))
    ge = np.zeros((LANES, 3 * R * LANES), np.float32)
    for rep in range(3):
        for r in range(R):
            for br in range(3):
                c0 = br * R * LANES + r * LANES
                ge[rep * 3 * R + 3 * r + br, c0:c0 + HEAD_DIM] = 1.0
    cp = np.zeros((R * LANES, R * HEAD_DIM), np.float32)
    for r in range(R):
        cp[r * LANES + np.arange(HEAD_DIM), r * HEAD_DIM + np.arange(HEAD_DIM)] = 1.0
    blk = lambda k: pl.BlockSpec((1, T, LANES), lambda b, g, i: (b, 0, 4 * g + k))
    return pl.pallas_call(
        functools.partial(_nsa_main_kernel, tq=tq, tk=tk, n_win=n_win),
        grid=(B, G, T // tq),
        in_specs=[
            pl.BlockSpec((1, tq, R * LANES), lambda b, g, i: (b, i, g)),
            pl.BlockSpec((1, 1, tq, LANES), lambda b, g, i: (b, g, i, 0)),
            blk(0), blk(1), blk(2), blk(3),
            pl.BlockSpec((1, nb, R * tq, tk), lambda b, g, i: (g, 0, 0, 0)),
            pl.BlockSpec((1, R * tq, LANES), lambda b, g, i: (g, 0, 0)),
            pl.BlockSpec((1, tq, LANES), lambda b, g, i: (b, i, g)),
            _const_spec(ge.shape),
            _const_spec(cp.shape),
            pl.BlockSpec((1, tq, R * LANES), lambda b, g, i: (b, i, g)),
        ],
        out_specs=pl.BlockSpec((1, tq, R * HEAD_DIM), lambda b, g, i: (b, i, g)),
        out_shape=jax.ShapeDtypeStruct((B, T, D), BF16),
        scratch_shapes=[pltpu.VMEM((R * tq, LANES), F32)] * 3,
        compiler_params=_cparams("arbitrary", "arbitrary", "arbitrary"),
        name="nsa_main",
    )(q, sel, kv, kv, kv, kv, bias, cfar, gates, jnp.asarray(ge, BF16), jnp.asarray(cp, BF16), o_cmp)


def kernel(x, c, rel_bias, ada_w, ada_b, attn_norm, mlp_norm, mlp_w1, mlp_w2, a_w_in, a_w_out, a_lambda, a_subln, kv_ada_w, kv_ada_b, kv_norm, w_kv, cmp_pos, cmp_w1, cmp_w2, b_w_in, b_w_out, final_norm):
    B, T, D = x.shape
    depth = ada_w.shape[0]
    n_a = a_w_in.shape[0]
    G, R, d = NSA_KV_HEADS, NSA_GROUP, HEAD_DIM

    mod = _adaln(c, ada_w, ada_b)
    kv_mod = _adaln(c, kv_ada_w[None], kv_ada_b[None])[0]

    wkv = w_kv.reshape(D, 6, G, d)
    z = jnp.zeros((D, G, d), w_kv.dtype)
    w_cmp = wkv[:, 0:2].reshape(D, 2 * G * d).astype(BF16)
    w_sw = jnp.stack([wkv[:, 2], z, wkv[:, 3], z, wkv[:, 4], wkv[:, 5], wkv[:, 5], z], axis=2)
    w_sw = w_sw.reshape(D, G * 4 * LANES).astype(BF16)
    kv_plan = [(0, c0, w, 0, c0, "plain") for c0, w in _chunks(2 * G * d)]
    kv_plan += [(1, c0, w, 1, c0, "kv_group") for c0, w in _chunks(G * 4 * LANES, 4 * LANES)]

    shared = None
    for layer in range(depth):
        sh_a, sc_a, gt_a, sh_m, sc_m, gt_m = jnp.split(mod[layer], 6, axis=-1)
        if layer < n_a:
            w_in = a_w_in[layer].astype(BF16)
            plan = [(0, c0, w, 0, c0, "scale" if c0 < D else "plain") for c0, w in _chunks(3 * D)]
            (qkv,) = _proj(x, attn_norm[layer], sh_a, sc_a, [w_in], [(3 * D, BF16)], plan, "diff_qkv_proj")
            mix = _diff_attention(qkv, rel_bias, a_lambda[layer], a_subln[layer], layer)
            w_out = a_w_out[layer].astype(BF16)
        else:
            i = layer - n_a
            w_q = b_w_in[i][:, :D].astype(BF16)
            w_g = jnp.tile(b_w_in[i][:, D:].reshape(D, G, 3 * R), (1, 1, 3))
            w_g = jnp.pad(w_g, ((0, 0), (0, 0), (0, LANES - 9 * R))).reshape(D, G * LANES).astype(BF16)
            plan = [(0, c0, w, 0, 2 * c0, "scale_pad") for c0, w in _chunks(D)]
            plan += [(1, 0, G * LANES, 1, 0, "sigmoid")]
            q, gates = _proj(x, attn_norm[layer], sh_a, sc_a, [w_q, w_g], [(2 * D, BF16), (G * LANES, F32)], plan,
                             "nsa_in_proj")
            kc, kv_sw = shared
            o_cmp, sel = _nsa_compressed_select(q, kc)
            mix = _nsa_main(q, sel, kv_sw, gates, o_cmp, rel_bias)
            w_out = b_w_out[i].astype(BF16)
        x = _mix_mlp_residual(x, mix, w_out, gt_a, mlp_norm[layer], sh_m, sc_m, gt_m, mlp_w1[layer].astype(BF16),
                              mlp_w2[layer].astype(BF16), final_norm, final=(layer == depth - 1))
        if layer == n_a - 1:
            sh_kv, sc_kv = jnp.split(kv_mod, 2, axis=-1)
            kv_cmp, kv_sw = _proj(x, kv_norm, sh_kv, sc_kv, [w_cmp, w_sw], [(2 * G * d, BF16), (G * 4 * LANES, BF16)],
                                  kv_plan, "nsa_kv_proj")
            shared = (_compress(kv_cmp, cmp_pos, cmp_w1, cmp_w2), kv_sw)
    return x
```

```python
import functools
import math

import numpy as np
import jax
import jax.numpy as jnp
from jax import lax
from jax.experimental import pallas as pl
from jax.experimental.pallas import tpu as pltpu

F32 = jnp.float32
BF16 = jnp.bfloat16

NEG = -1e30
BELOW_NEG = -3e38
NORM_EPS = 1e-6

HEAD_DIM = 64
DIFF_HEADS = 8
NSA_KV_HEADS = 4
NSA_GROUP = 4
REL_BUCKETS = 32
REL_MAX_DIST = 128
CMP_LEN = 32
CMP_STRIDE = 16
SLC_LEN = 64
SLC_TOPK = 16
SLC_FORCED_LOCAL = 2
FORCE_BONUS = 1e4
WINDOW = 512

LANES = 128
VMEM_LIMIT_BYTES = 56 * 1024 * 1024

PROJ_ROWS = 512
DIFF_TQ = 512
DIFF_TK = 256
FAR_UNROLL_LOG2 = 3
FAR_UNROLL = 1 << FAR_UNROLL_LOG2
LOG2E = math.log2(math.e)
NSA_TQ = 256
CMP_TQ = 1024
FF_CHUNK = 1024


def _cparams(*sem):
    return pltpu.CompilerParams(dimension_semantics=sem, vmem_limit_bytes=VMEM_LIMIT_BYTES)


def _const_spec(shape):
    nd = len(shape)
    return pl.BlockSpec(shape, lambda *_: (0,) * nd, pipeline_mode=pl.Buffered(1))


def _adaln_kernel(c_ref, w_ref, b_ref, o_ref):
    c = c_ref[...]
    c_act = (c * (1.0 / (1.0 + jnp.exp(-c)))).astype(BF16)
    o_ref[0] = jnp.dot(c_act, w_ref[0].astype(BF16), preferred_element_type=F32) + b_ref[0]


def _adaln(c, w, b, tn=2048):
    L, D, N = w.shape
    B = c.shape[0]
    tn = min(tn, N)
    return pl.pallas_call(
        _adaln_kernel,
        grid=(L, N // tn),
        in_specs=[
            pl.BlockSpec((B, D), lambda l, j: (0, 0)),
            pl.BlockSpec((1, D, tn), lambda l, j: (l, 0, j)),
            pl.BlockSpec((1, 1, tn), lambda l, j: (l, 0, j)),
        ],
        out_specs=pl.BlockSpec((1, B, tn), lambda l, j: (l, 0, j)),
        out_shape=jax.ShapeDtypeStruct((L, B, N), F32),
        compiler_params=_cparams("arbitrary", "arbitrary"),
        name="adaln",
    )(c, w, b.reshape(L, 1, N))


def _rms_mod(x, g, shift, scale):
    ms = jnp.mean(x * x, axis=-1, keepdims=True)
    y = x * lax.rsqrt(ms + NORM_EPS) * g
    return y * (1.0 + scale) + shift


def _proj_kernel(x_ref, g_ref, sh_ref, sc_ref, *refs, n_w, plan, rows):
    w_refs, o_refs = refs[:n_w], refs[n_w:]
    h = _rms_mod(x_ref[0], g_ref[...], sh_ref[0], sc_ref[0]).astype(BF16)
    for (wi, c0, width, oi, o0, mode) in plan:
        y = jnp.dot(h, w_refs[wi][:, c0:c0 + width], preferred_element_type=F32)
        if mode in ("scale", "scale_pad"):
            y = y * (HEAD_DIM ** -0.5 * LOG2E)
        elif mode == "sigmoid":
            y = 1.0 / (1.0 + jnp.exp(-y))
        elif mode == "kv_group":
            pos = pl.program_id(1) * rows + lax.broadcasted_iota(jnp.int32, y.shape, 0)
            col = lax.broadcasted_iota(jnp.int32, y.shape, 1)
            blk = col // LANES
            upper = col % LANES - HEAD_DIM
            hot = (blk == 0) & (upper == lax.shift_right_logical(pos, int(math.log2(SLC_LEN))))
            ones = ((blk == 1) | (blk == 3)) & (upper >= 0)
            y = y + jnp.where(hot | ones, 1.0, 0.0)
        y = y.astype(o_refs[oi].dtype)
        if mode == "scale_pad":
            zeros = jnp.zeros((y.shape[0], HEAD_DIM), y.dtype)
            pieces = []
            for c in range(0, width, HEAD_DIM):
                pieces += [y[:, c:c + HEAD_DIM], zeros]
            y = jnp.concatenate(pieces, axis=1)
        o_refs[oi][0, :, o0:o0 + y.shape[1]] = y


def _proj(x, g, shift, scale, weights, outs, plan, name):
    B, T, D = x.shape
    rows = min(PROJ_ROWS, T)
    in_specs = [
        pl.BlockSpec((1, rows, D), lambda b, i: (b, i, 0)),
        _const_spec((1, D)),
        pl.BlockSpec((1, 1, D), lambda b, i: (b, 0, 0)),
        pl.BlockSpec((1, 1, D), lambda b, i: (b, 0, 0)),
    ] + [_const_spec(w.shape) for w in weights]
    out_specs = [pl.BlockSpec((1, rows, n), lambda b, i: (b, i, 0)) for n, _ in outs]
    out_shape = [jax.ShapeDtypeStruct((B, T, n), dt) for n, dt in outs]
    res = pl.pallas_call(
        functools.partial(_proj_kernel, n_w=len(weights), plan=tuple(plan), rows=rows),
        grid=(B, T // rows),
        in_specs=in_specs,
        out_specs=out_specs,
        out_shape=out_shape,
        compiler_params=_cparams("arbitrary", "arbitrary"),
        name=name,
    )(x, g.reshape(1, D), shift.reshape(B, 1, D), scale.reshape(B, 1, D), *weights)
    return res


def _chunks(n, step=512):
    return [(c, min(step, n - c)) for c in range(0, n, step)]


def _mix_mlp_kernel(x_ref, o_ref, wo_ref, ga_ref, g_ref, sh_ref, sc_ref, gm_ref, w1_ref, w2_ref, fg_ref, y_ref,
                    *, d_ff, final):
    x = x_ref[0] + ga_ref[0] * jnp.dot(o_ref[0], wo_ref[...], preferred_element_type=F32)
    h = _rms_mod(x, g_ref[...], sh_ref[0], sc_ref[0]).astype(BF16)
    acc = jnp.zeros(x.shape, F32)
    for c0 in range(0, d_ff, FF_CHUNK):
        a = jnp.dot(h, w1_ref[:, c0:c0 + FF_CHUNK], preferred_element_type=F32)
        a = jnp.square(jnp.maximum(a, 0.0)).astype(BF16)
        acc = acc + jnp.dot(a, w2_ref[c0:c0 + FF_CHUNK, :], preferred_element_type=F32)
    y = x + gm_ref[0] * acc
    if final:
        ms = jnp.mean(y * y, axis=-1, keepdims=True)
        y = y * lax.rsqrt(ms + NORM_EPS) * fg_ref[...]
    y_ref[0] = y


def _mix_mlp_residual(x, o, w_out, gate_a, g, shift, scale, gate_m, w1, w2, final_gain, final):
    B, T, D = x.shape
    d_ff = w1.shape[1]
    rows = min(PROJ_ROWS, T)
    tile = pl.BlockSpec((1, rows, D), lambda b, i: (b, i, 0))
    vec = pl.BlockSpec((1, 1, D), lambda b, i: (b, 0, 0))
    per_batch = lambda v: v.reshape(B, 1, D)
    return pl.pallas_call(
        functools.partial(_mix_mlp_kernel, d_ff=d_ff, final=final),
        grid=(B, T // rows),
        in_specs=[tile, tile, _const_spec(w_out.shape), vec,
                  _const_spec((1, D)), vec, vec, vec,
                  _const_spec(w1.shape), _const_spec(w2.shape), _const_spec((1, D))],
        out_specs=tile,
        out_shape=jax.ShapeDtypeStruct((B, T, D), F32),
        compiler_params=_cparams("arbitrary", "arbitrary"),
        name="mix_mlp_residual",
    )(x, o, w_out, per_batch(gate_a), g.reshape(1, D), per_batch(shift), per_batch(scale), per_batch(gate_m),
      w1, w2, final_gain.reshape(1, D))


def _t5_bucket(dist):
    n = jnp.maximum(dist, 0)
    max_exact = REL_BUCKETS // 2
    nf = jnp.maximum(n, 1).astype(F32)
    large = max_exact + (jnp.log(nf / max_exact) / math.log(REL_MAX_DIST / max_exact)
                         * (REL_BUCKETS - max_exact)).astype(jnp.int32)
    large = jnp.minimum(large, REL_BUCKETS - 1)
    return jnp.where(n < max_exact, n, large)


def _bias_tiles(rel_bias, rels, tq, tk, window):
    iq = np.arange(tq)[:, None]
    ik = np.arange(tk)[None, :]
    dist = jnp.asarray(np.stack([r + iq - ik for r in rels]).astype(np.int32))
    hot = (_t5_bucket(dist)[..., None] == jnp.arange(REL_BUCKETS)).astype(F32)
    b = jnp.einsum("ntkb,bh->hntk", hot, rel_bias.astype(F32), precision=lax.Precision.HIGHEST)
    ok = dist >= 0
    if window is not None:
        ok = ok & (dist < window)
    tiles = jnp.where(ok[None], b * LOG2E, NEG)
    return jnp.concatenate([tiles, jnp.full_like(tiles[:, :1], NEG)], axis=1)


def _attend(q, tiles, stats):
    m, l, acc = stats
    for tile in tiles:
        k, v, bias = tile()
        s = lax.dot_general(q, k, (((1,), (1,)), ((), ())), preferred_element_type=F32)
        parts = [s[:, i:i + LANES] for i in range(0, s.shape[1], LANES)]
        if bias is not None:
            cols = [bias[:, i:i + LANES] for i in range(0, bias.shape[1], LANES)]
            parts = [t + cols[i % len(cols)] for i, t in enumerate(parts)]
        mx = functools.reduce(jnp.maximum, parts)
        m_new = jnp.maximum(m, jnp.max(mx, axis=-1, keepdims=True))
        alpha = jnp.exp2(m - m_new)
        p = [jnp.exp2(t - m_new) for t in parts]
        if l is not None:
            l = alpha * l + functools.reduce(jnp.add, p)
        pv = jnp.dot(jnp.concatenate(p, axis=1).astype(BF16), v, preferred_element_type=F32)
        acc = alpha * acc + pv
        m = m_new
    return m, l, acc


def _fresh_stats(rows, row_sums=True):
    zeros = jnp.zeros((rows, LANES), F32)
    return (jnp.full((rows, LANES), NEG, F32), zeros if row_sums else None, zeros)


def _causal_sweep(q, near, n_far, load_tile, cfar, scratch, side_work=None, pre_work=None, active=True):
    m_sc, l_sc, acc_sc = scratch
    rows = q.shape[0]
    n_groups = lax.shift_right_logical(n_far, FAR_UNROLL_LOG2)
    rem = n_far - n_groups * FAR_UNROLL

    def save(m, l, acc):
        m_sc[...], acc_sc[...] = m, acc
        if l_sc is not None:
            l_sc[...] = l

    for r in range(FAR_UNROLL):
        @pl.when((rem == r) & active)
        def _():
            if pre_work is not None:
                pre_work()
            extra = [functools.partial(load_tile, n_groups * FAR_UNROLL + i, cfar) for i in range(r)]
            m, l, acc = _attend(q, near + extra, _fresh_stats(rows, l_sc is not None))
            save(m - cfar(), l, acc)
            if side_work is not None:
                side_work()

    def group(g, carry):
        tiles = [functools.partial(load_tile, g * FAR_UNROLL + u, None) for u in range(FAR_UNROLL)]
        save(*_attend(q, tiles, (m_sc[...], None if l_sc is None else l_sc[...], acc_sc[...])))
        return carry

    lax.fori_loop(0, n_groups, group, 0)


def _diff_attn_kernel(q_ref, k_ref, v_ref, bias_ref, cfar_ref, lam_ref, subln_ref, o_ref,
                      m_sc, l_sc, acc_sc, *, tq, tk, lam_init):
    qi = pl.program_id(2)
    q = q_ref[0]
    lane = lax.broadcasted_iota(jnp.int32, q.shape, 1)
    zero = jnp.zeros_like(q)
    q2 = jnp.concatenate([jnp.where(lane < HEAD_DIM, q, zero),
                          jnp.where(lane >= HEAD_DIM, q, zero)], axis=0)

    def load_tile(j, bias):
        return (k_ref[0, pl.ds(j * tk, tk), :], v_ref[0, pl.ds(j * tk, tk), :], bias() if bias else None)

    ratio = tq // tk
    j0 = qi * ratio
    near = [functools.partial(load_tile, j0 + r, lambda r=r: bias_ref[0, r]) for r in range(ratio)]
    near.append(functools.partial(load_tile, jnp.maximum(j0 - 1, 0),
                                  lambda: bias_ref[0, jnp.where(qi >= 1, ratio, ratio + 1)]))
    _causal_sweep(q2, near, jnp.maximum(j0 - 1, 0), load_tile, lambda: cfar_ref[0], (m_sc, l_sc, acc_sc))

    lf = lam_ref[...]
    lam = (jnp.exp(jnp.sum(lf[0:1] * lf[1:2], axis=-1, keepdims=True))
           - jnp.exp(jnp.sum(lf[2:3] * lf[3:4], axis=-1, keepdims=True)) + lam_init)
    on = acc_sc[...] / jnp.sum(l_sc[...], axis=-1, keepdims=True)
    o = on[:tq] - lam * on[tq:]
    ms = jnp.mean(o * o, axis=-1, keepdims=True)
    o = o * lax.rsqrt(ms + NORM_EPS) * subln_ref[...] * (1.0 - lam_init)
    o_ref[0] = o.astype(o_ref.dtype)


def _diff_attention(qkv, rel_bias, lam, subln, layer_idx):
    B, T, D3 = qkv.shape
    D = D3 // 3
    H = DIFF_HEADS
    tq = min(DIFF_TQ, T)
    tk = min(DIFF_TK, tq)
    ratio = tq // tk
    lam_init = 0.8 - 0.6 * math.exp(-0.3 * layer_idx)
    assert 2 * tk - (tk - 1) >= REL_MAX_DIST
    rels = [-r * tk for r in range(ratio)] + [tk]
    nb = len(rels) + 1
    bt = _bias_tiles(rel_bias, rels, tq, tk, None)
    bias = bt.reshape(H, 2, nb, tq, tk).transpose(0, 2, 1, 3, 4).reshape(H, nb, 2 * tq, tk)
    cfar = jnp.repeat(rel_bias.astype(F32)[REL_BUCKETS - 1].reshape(H, 2) * LOG2E, tq, axis=1)
    cfar = jnp.broadcast_to(cfar.reshape(H, 2 * tq, 1), (H, 2 * tq, LANES))
    hb = D // LANES
    return pl.pallas_call(
        functools.partial(_diff_attn_kernel, tq=tq, tk=tk, lam_init=lam_init),
        grid=(B, H, T // tq),
        in_specs=[
            pl.BlockSpec((1, tq, LANES), lambda b, h, i: (b, i, h)),
            pl.BlockSpec((1, T, LANES), lambda b, h, i: (b, 0, hb + h)),
            pl.BlockSpec((1, T, LANES), lambda b, h, i: (b, 0, 2 * hb + h)),
            pl.BlockSpec((1, nb, 2 * tq, tk), lambda b, h, i: (h, 0, 0, 0)),
            pl.BlockSpec((1, 2 * tq, LANES), lambda b, h, i: (h, 0, 0)),
            _const_spec(lam.shape),
            _const_spec((1, LANES)),
        ],
        out_specs=pl.BlockSpec((1, tq, LANES), lambda b, h, i: (b, i, h)),
        out_shape=jax.ShapeDtypeStruct((B, T, D), BF16),
        scratch_shapes=[pltpu.VMEM((2 * tq, LANES), F32)] * 3,
        compiler_params=_cparams("arbitrary", "arbitrary", "arbitrary"),
        name="diff_attention",
    )(qkv, qkv, qkv, bias, cfar, lam.astype(F32), subln.astype(F32).reshape(1, LANES))


def _compress_kernel(a_ref, w1_ref, pos_ref, w2_ref, o_ref):
    half = CMP_STRIDE * HEAD_DIM
    for s in range(2):
        a = a_ref[s, 0, 0]
        w1 = w1_ref[s]
        p = jnp.dot(a, w1[:half], preferred_element_type=F32)
        q = jnp.dot(a, w1[half:], preferred_element_type=F32)
        posb = jnp.dot(jnp.broadcast_to(pos_ref[s], (8, 2 * half)).astype(BF16), w1,
                       preferred_element_type=F32)[0:1]
        hid = p + pltpu.roll(q, q.shape[0] - 1, axis=0) + posb
        hid = 0.5 * hid * (1.0 + jnp.tanh(math.sqrt(2.0 / math.pi) * (hid + 0.044715 * hid * hid * hid)))
        c = jnp.dot(hid.astype(BF16), w2_ref[s], preferred_element_type=F32)
        o_ref[0, 0, s] = jnp.concatenate([c, jnp.zeros_like(c)], axis=1).astype(o_ref.dtype)


def _compress(kv_cmp, cmp_pos, cmp_w1, cmp_w2):
    B, T, _ = kv_cmp.shape
    G, d = NSA_KV_HEADS, HEAD_DIM
    n = T // CMP_STRIDE
    a = kv_cmp.reshape(B, n, CMP_STRIDE, 2, G, d).transpose(3, 0, 4, 1, 2, 5).reshape(2, B, G, n, CMP_STRIDE * d)
    hid = cmp_w1.shape[-1]
    return pl.pallas_call(
        _compress_kernel,
        grid=(B, G),
        in_specs=[
            pl.BlockSpec((2, 1, 1, n, CMP_STRIDE * d), lambda b, g: (0, b, g, 0, 0)),
            _const_spec((2, CMP_LEN * d, hid)),
            _const_spec((2, 1, CMP_LEN * d)),
            _const_spec((2, hid, d)),
        ],
        out_specs=pl.BlockSpec((1, 1, 2, n, LANES), lambda b, g: (b, g, 0, 0, 0)),
        out_shape=jax.ShapeDtypeStruct((B, G, 2, n, LANES), BF16),
        compiler_params=_cparams("arbitrary", "arbitrary"),
        name="nsa_compress",
    )(a, cmp_w1.astype(BF16), cmp_pos.astype(F32).reshape(2, 1, CMP_LEN * d), cmp_w2.astype(BF16))


def _rows_from_lane_blocks(x):
    return jnp.concatenate([x[:, r * LANES:(r + 1) * LANES] for r in range(NSA_GROUP)], axis=0)


def _lane_blocks_from_rows(x):
    tq = x.shape[0] // NSA_GROUP
    return jnp.concatenate([x[r * tq:(r + 1) * tq] for r in range(NSA_GROUP)], axis=1)


def _nsa_cmp_kernel(q_ref, kc_ref, ov_ref, tri_ref, oc_ref, sel_ref, *, tq, top_k):
    oc, sel = _cmp_select(q_ref[0], pl.program_id(2) * tq, kc_ref[0, 0, 0], kc_ref[0, 0, 1],
                          ov_ref[...], tri_ref[...], top_k)
    oc_ref[0] = oc.astype(oc_ref.dtype)
    sel_ref[0, 0] = sel.astype(sel_ref.dtype)


def _cmp_select(q, q0, k_c, v_c, ov, tri, top_k):
    R = NSA_GROUP
    tq = q.shape[0]
    n_cmp = k_c.shape[0]
    q4 = _rows_from_lane_blocks(q)
    s = lax.dot_general(q4, k_c, (((1,), (1,)), ((), ())), preferred_element_type=F32)
    qpos = q0 + lax.broadcasted_iota(jnp.int32, (tq, 1), 0)
    cmp_end = lax.broadcasted_iota(jnp.int32, (1, n_cmp), 1) * CMP_STRIDE + (CMP_LEN - 1)
    valid4 = jnp.concatenate([cmp_end <= qpos] * R, axis=0)
    sm = jnp.where(valid4, s, NEG)
    e = jnp.exp2(sm - jnp.max(sm, axis=-1, keepdims=True))
    any_valid = jnp.concatenate([qpos >= CMP_LEN - 1] * R, axis=0)
    p = e * jnp.where(any_valid, 1.0 / jnp.sum(e, axis=-1, keepdims=True), 0.0)
    oc = _lane_blocks_from_rows(jnp.dot(p.astype(BF16), v_c, preferred_element_type=F32))

    psum = p[0:tq]
    for r in range(1, R):
        psum = psum + p[r * tq:(r + 1) * tq]
    hi = psum.astype(BF16)
    lo = (psum - hi.astype(F32)).astype(BF16)
    imp = jnp.dot(hi, ov, preferred_element_type=F32) + jnp.dot(lo, ov, preferred_element_type=F32)
    j = lax.broadcasted_iota(jnp.int32, (1, LANES), 1)
    qblk = lax.shift_right_logical(qpos, int(math.log2(SLC_LEN)))
    forced = (j == 0) | ((j <= qblk) & (j > qblk - SLC_FORCED_LOCAL))
    imp = jnp.where(forced, FORCE_BONUS, imp)
    imp = jnp.where(j <= qblk, imp, NEG)

    n_slc = tri.shape[0]
    vals = imp.T[:n_slc]
    rem = vals
    cnt = jnp.zeros((1, tq), F32)
    thr = jnp.full((1, tq), NEG, F32)
    for _ in range(top_k):
        mx = jnp.max(rem, axis=0, keepdims=True)
        active = cnt < top_k
        thr = jnp.where(active, mx, thr)
        hit = rem == mx
        cnt = cnt + jnp.where(active, jnp.sum(jnp.where(hit, 1.0, 0.0), axis=0, keepdims=True), 0.0)
        rem = jnp.where(hit, BELOW_NEG, rem)
    above = vals > thr
    n_above = jnp.sum(jnp.where(above, 1.0, 0.0), axis=0, keepdims=True)
    tie = vals == thr
    prefix = jnp.dot(tri, jnp.where(tie, 1.0, 0.0).astype(BF16), preferred_element_type=F32)
    chosen = (above | (tie & (prefix <= top_k - n_above))) & (vals > NEG)
    selb = jnp.where(chosen, 0.0, NEG)
    pieces = [jnp.zeros((HEAD_DIM, tq), F32), selb]
    if HEAD_DIM - n_slc:
        pieces.append(jnp.zeros((HEAD_DIM - n_slc, tq), F32))
    return oc, jnp.concatenate(pieces, axis=0).T


def _nsa_compressed_select(q, kc):
    B, T, _ = q.shape
    G, R = NSA_KV_HEADS, NSA_GROUP
    n_cmp = kc.shape[3]
    n_slc = T // SLC_LEN
    assert n_slc <= HEAD_DIM, "selection mask is carried in the 64 spare contraction lanes"
    top_k = min(SLC_TOPK, n_slc)
    tq = min(CMP_TQ, T)
    nn = np.arange(n_cmp)[:, None]
    jj = np.arange(LANES)[None, :]
    overlap = ((nn * CMP_STRIDE < jj * SLC_LEN + SLC_LEN) & (nn * CMP_STRIDE + CMP_LEN - 1 >= jj * SLC_LEN)
               & (jj < n_slc))
    overlap = overlap & (nn < (T - CMP_LEN) // CMP_STRIDE + 1)
    tri = np.tril(np.ones((n_slc, n_slc), np.float32))
    return pl.pallas_call(
        functools.partial(_nsa_cmp_kernel, tq=tq, top_k=top_k),
        grid=(B, G, T // tq),
        in_specs=[
            pl.BlockSpec((1, tq, R * LANES), lambda b, g, i: (b, i, g)),
            pl.BlockSpec((1, 1, 2, n_cmp, LANES), lambda b, g, i: (b, g, 0, 0, 0)),
            _const_spec((n_cmp, LANES)),
            _const_spec((n_slc, n_slc)),
        ],
        out_specs=[
            pl.BlockSpec((1, tq, R * LANES), lambda b, g, i: (b, i, g)),
            pl.BlockSpec((1, 1, tq, LANES), lambda b, g, i: (b, g, i, 0)),
        ],
        out_shape=[jax.ShapeDtypeStruct((B, T, G * R * LANES), BF16), jax.ShapeDtypeStruct((B, G, T, LANES), BF16)],
        compiler_params=_cparams("arbitrary", "arbitrary", "arbitrary"),
        name="nsa_compressed_select",
    )(q, kc, jnp.asarray(overlap.astype(np.float32), BF16), jnp.asarray(tri, BF16))


def _nsa_main_kernel(q_ref, sel_ref, ka_ref, vs_ref, kw_ref, vw_ref, bias_ref, cfar_ref, g_ref, ge_ref, cp_ref,
                     oc_ref, o_ref, ms_sc, as_sc, aw_sc, *, tq, tk, n_win, nq):
    qi = pl.program_id(2)
    R = NSA_GROUP
    slot = lax.rem(qi, 2)
    as_cur, aw_cur = as_sc.at[slot], aw_sc.at[slot]
    as_prev, aw_prev = as_sc.at[1 - slot], aw_sc.at[1 - slot]

    @pl.when((pl.program_id(0) == 0) & (pl.program_id(1) == 0) & (qi == 0))
    def _():
        as_sc[...] = jnp.ones(as_sc.shape, F32)
        aw_sc[...] = jnp.ones(aw_sc.shape, F32)

    q4w = _rows_from_lane_blocks(q_ref[0])
    q4s = q4w + jnp.concatenate([sel_ref[0, 0]] * R, axis=0)

    def load_s(j, bias):
        return (ka_ref[0, pl.ds(j * tk, tk), :], vs_ref[0, pl.ds(j * tk, tk), :], bias() if bias else None)

    def load_w(j, bias):
        return (kw_ref[0, pl.ds(j * tk, tk), :], vw_ref[0, pl.ds(j * tk, tk), :], bias() if bias else None)

    near_s, near_w = [], []
    for t in range(n_win):
        j = jnp.maximum(qi - t, 0)
        b = (lambda: bias_ref[0, 0]) if t == 0 else (lambda t=t: bias_ref[0, jnp.where(qi >= t, t, n_win)])
        near_w.append(functools.partial(load_w, j, b))
        if t <= 1:
            near_s.append(functools.partial(load_s, j, b))

    def window_branch():
        _, _, aw_cur[...] = _attend(q4w, near_w, _fresh_stats(R * tq, row_sums=False))

    def normalised(acc):
        return acc / jnp.max(acc[:, HEAD_DIM:], axis=-1, keepdims=True)

    def finish_previous():
        o_s = _lane_blocks_from_rows(normalised(as_prev[...]))
        o_w = _lane_blocks_from_rows(normalised(aw_prev[...]))
        g = g_ref[0]
        lane = lax.broadcasted_iota(jnp.int32, g.shape, 1)
        hi = g.astype(BF16)
        r1 = g - hi.astype(F32)
        mid = r1.astype(BF16)
        lo = (r1 - mid.astype(F32)).astype(BF16)
        n_g = 3 * R
        g_split = jnp.where(lane < n_g, hi, jnp.where(lane < 2 * n_g, mid, lo))
        ge = jnp.dot(g_split, ge_ref[...], preferred_element_type=F32)
        w = R * LANES
        o = ge[:, :w] * oc_ref[0].astype(F32) + ge[:, w:2 * w] * o_s + ge[:, 2 * w:] * o_w
        o_ref[0] = jnp.dot(o.astype(BF16), cp_ref[...], preferred_element_type=F32).astype(o_ref.dtype)

    live = qi < nq
    _causal_sweep(q4s, near_s, jnp.where(live, jnp.maximum(qi - 1, 0), 0), load_s, lambda: cfar_ref[0],
                  (ms_sc, None, as_cur), side_work=window_branch, pre_work=finish_previous, active=live)

    @pl.when(qi == nq)
    def _():
        finish_previous()


def _nsa_main(q, sel, kv, gates, o_cmp, rel_bias):
    B, T, _ = q.shape
    G, R = NSA_KV_HEADS, NSA_GROUP
    D = G * R * HEAD_DIM
    tq = min(NSA_TQ, T)
    tk = tq
    n_win = (WINDOW - 2) // tk + 2
    rels = [t * tk for t in range(n_win)]
    assert tk + tq - 1 < WINDOW and 2 * tk - (tk - 1) >= REL_MAX_DIST
    bt = _bias_tiles(rel_bias, rels, tq, tk, WINDOW)
    nb = n_win + 1
    bias = bt.reshape(G, R, nb, tq, tk).transpose(0, 2, 1, 3, 4).reshape(G, nb, R * tq, tk)
    cfar = jnp.repeat(rel_bias.astype(F32)[REL_BUCKETS - 1].reshape(G, R) * LOG2E, tq, axis=1)
    cfar = jnp.broadcast_to(cfar.reshape(G, R * tq, 1), (G, R * tq, LANES))
    ge = np.zeros((LANES, 3 * R * LANES), np.float32)
    for rep in range(3):
        for r in range(R):
            for br in range(3):
                c0 = br * R * LANES + r * LANES
                ge[rep * 3 * R + 3 * r + br, c0:c0 + HEAD_DIM] = 1.0
    cp = np.zeros((R * LANES, R * HEAD_DIM), np.float32)
    for r in range(R):
        cp[r * LANES + np.arange(HEAD_DIM), r * HEAD_DIM + np.arange(HEAD_DIM)] = 1.0
    blk = lambda k: pl.BlockSpec((1, T, LANES), lambda b, g, i: (b, 0, 4 * g + k))
    nq = T // tq
    swept = lambda i: jnp.minimum(i, nq - 1)
    done = lambda i: jnp.maximum(i - 1, 0)
    stats = pltpu.VMEM((2, R * tq, LANES), F32)
    return pl.pallas_call(
        functools.partial(_nsa_main_kernel, tq=tq, tk=tk, n_win=n_win, nq=nq),
        grid=(B, G, nq + 1),
        in_specs=[
            pl.BlockSpec((1, tq, R * LANES), lambda b, g, i: (b, swept(i), g)),
            pl.BlockSpec((1, 1, tq, LANES), lambda b, g, i: (b, g, swept(i), 0)),
            blk(0), blk(1), blk(2), blk(3),
            pl.BlockSpec((1, nb, R * tq, tk), lambda b, g, i: (g, 0, 0, 0)),
            pl.BlockSpec((1, R * tq, LANES), lambda b, g, i: (g, 0, 0)),
            pl.BlockSpec((1, tq, LANES), lambda b, g, i: (b, done(i), g)),
            _const_spec(ge.shape),
            _const_spec(cp.shape),
            pl.BlockSpec((1, tq, R * LANES), lambda b, g, i: (b, done(i), g)),
        ],
        out_specs=pl.BlockSpec((1, tq, R * HEAD_DIM), lambda b, g, i: (b, done(i), g)),
        out_shape=jax.ShapeDtypeStruct((B, T, D), BF16),
        scratch_shapes=[pltpu.VMEM((R * tq, LANES), F32), stats, stats],
        compiler_params=_cparams("arbitrary", "arbitrary", "arbitrary"),
        name="nsa_main",
    )(q, sel, kv, kv, kv, kv, bias, cfar, gates, jnp.asarray(ge, BF16), jnp.asarray(cp, BF16), o_cmp)


def kernel(x, c, rel_bias, ada_w, ada_b, attn_norm, mlp_norm, mlp_w1, mlp_w2, a_w_in, a_w_out, a_lambda, a_subln, kv_ada_w, kv_ada_b, kv_norm, w_kv, cmp_pos, cmp_w1, cmp_w2, b_w_in, b_w_out, final_norm):
    B, T, D = x.shape
    depth = ada_w.shape[0]
    n_a = a_w_in.shape[0]
    G, R, d = NSA_KV_HEADS, NSA_GROUP, HEAD_DIM

    mod = _adaln(c, ada_w, ada_b)
    kv_mod = _adaln(c, kv_ada_w[None], kv_ada_b[None])[0]

    wkv = w_kv.reshape(D, 6, G, d)
    z = jnp.zeros((D, G, d), w_kv.dtype)
    w_cmp = wkv[:, 0:2].reshape(D, 2 * G * d).astype(BF16)
    w_sw = jnp.stack([wkv[:, 2], z, wkv[:, 3], z, wkv[:, 4], wkv[:, 5], wkv[:, 5], z], axis=2)
    w_sw = w_sw.reshape(D, G * 4 * LANES).astype(BF16)
    kv_plan = [(0, c0, w, 0, c0, "plain") for c0, w in _chunks(2 * G * d)]
    kv_plan += [(1, c0, w, 1, c0, "kv_group") for c0, w in _chunks(G * 4 * LANES, 4 * LANES)]

    shared = None
    for layer in range(depth):
        sh_a, sc_a, gt_a, sh_m, sc_m, gt_m = jnp.split(mod[layer], 6, axis=-1)
        if layer < n_a:
            w_in = a_w_in[layer].astype(BF16)
            plan = [(0, c0, w, 0, c0, "scale" if c0 < D else "plain") for c0, w in _chunks(3 * D)]
            (qkv,) = _proj(x, attn_norm[layer], sh_a, sc_a, [w_in], [(3 * D, BF16)], plan, "diff_qkv_proj")
            mix = _diff_attention(qkv, rel_bias, a_lambda[layer], a_subln[layer], layer)
            w_out = a_w_out[layer].astype(BF16)
        else:
            i = layer - n_a
            w_q = b_w_in[i][:, :D].astype(BF16)
            w_g = jnp.tile(b_w_in[i][:, D:].reshape(D, G, 3 * R), (1, 1, 3))
            w_g = jnp.pad(w_g, ((0, 0), (0, 0), (0, LANES - 9 * R))).reshape(D, G * LANES).astype(BF16)
            plan = [(0, c0, w, 0, 2 * c0, "scale_pad") for c0, w in _chunks(D)]
            plan += [(1, 0, G * LANES, 1, 0, "sigmoid")]
            q, gates = _proj(x, attn_norm[layer], sh_a, sc_a, [w_q, w_g], [(2 * D, BF16), (G * LANES, F32)], plan,
                             "nsa_in_proj")
            kc, kv_sw = shared
            o_cmp, sel = _nsa_compressed_select(q, kc)
            mix = _nsa_main(q, sel, kv_sw, gates, o_cmp, rel_bias)
            w_out = b_w_out[i].astype(BF16)
        x = _mix_mlp_residual(x, mix, w_out, gt_a, mlp_norm[layer], sh_m, sc_m, gt_m, mlp_w1[layer].astype(BF16),
                              mlp_w2[layer].astype(BF16), final_norm, final=(layer == depth - 1))
        if layer == n_a - 1:
            sh_kv, sc_kv = jnp.split(kv_mod, 2, axis=-1)
            kv_cmp, kv_sw = _proj(x, kv_norm, sh_kv, sc_kv, [w_cmp, w_sw], [(2 * G * d, BF16), (G * 4 * LANES, BF16)],
                                  kv_plan, "nsa_kv_proj")
            shared = (_compress(kv_cmp, cmp_pos, cmp_w1, cmp_w2), kv_sw)
    return x
```

```python
import functools
import math

import numpy as np
import jax
import jax.numpy as jnp
from jax import lax
from jax.experimental import pallas as pl
from jax.experimental.pallas import tpu as pltpu

F32 = jnp.float32
BF16 = jnp.bfloat16

NEG = -1e30
BELOW_NEG = -3e38
NORM_EPS = 1e-6

HEAD_DIM = 64
DIFF_HEADS = 8
NSA_KV_HEADS = 4
NSA_GROUP = 4
REL_BUCKETS = 32
REL_MAX_DIST = 128
CMP_LEN = 32
CMP_STRIDE = 16
SLC_LEN = 64
SLC_TOPK = 16
SLC_FORCED_LOCAL = 2
FORCE_BONUS = 1e4
WINDOW = 512

LANES = 128
VMEM_LIMIT_BYTES = 56 * 1024 * 1024

PROJ_ROWS = 512
DIFF_TQ = 512
DIFF_TK = 256
FAR_UNROLL_LOG2 = 3
FAR_UNROLL = 1 << FAR_UNROLL_LOG2
LOG2E = math.log2(math.e)
NSA_TQ = 256
CMP_TQ = 1024
FF_CHUNK = 1024


def _cparams(*sem):
    return pltpu.CompilerParams(dimension_semantics=sem, vmem_limit_bytes=VMEM_LIMIT_BYTES)


def _const_spec(shape):
    nd = len(shape)
    return pl.BlockSpec(shape, lambda *_: (0,) * nd, pipeline_mode=pl.Buffered(1))


def _adaln_kernel(c_ref, w_ref, b_ref, o_ref):
    c = c_ref[...]
    c_act = (c * (1.0 / (1.0 + jnp.exp(-c)))).astype(BF16)
    o_ref[0] = jnp.dot(c_act, w_ref[0].astype(BF16), preferred_element_type=F32) + b_ref[0]


def _adaln(c, w, b, tn=2048):
    L, D, N = w.shape
    B = c.shape[0]
    tn = min(tn, N)
    return pl.pallas_call(
        _adaln_kernel,
        grid=(L, N // tn),
        in_specs=[
            pl.BlockSpec((B, D), lambda l, j: (0, 0)),
            pl.BlockSpec((1, D, tn), lambda l, j: (l, 0, j)),
            pl.BlockSpec((1, 1, tn), lambda l, j: (l, 0, j)),
        ],
        out_specs=pl.BlockSpec((1, B, tn), lambda l, j: (l, 0, j)),
        out_shape=jax.ShapeDtypeStruct((L, B, N), F32),
        compiler_params=_cparams("arbitrary", "arbitrary"),
        name="adaln",
    )(c, w, b.reshape(L, 1, N))


def _rms_mod(x, g, shift, scale):
    ms = jnp.mean(x * x, axis=-1, keepdims=True)
    y = x * lax.rsqrt(ms + NORM_EPS) * g
    return y * (1.0 + scale) + shift


def _proj_kernel(x_ref, g_ref, sh_ref, sc_ref, *refs, n_w, plan, rows):
    w_refs, o_refs = refs[:n_w], refs[n_w:]
    h = _rms_mod(x_ref[0], g_ref[...], sh_ref[0], sc_ref[0]).astype(BF16)
    for (wi, c0, width, oi, o0, mode) in plan:
        y = jnp.dot(h, w_refs[wi][:, c0:c0 + width], preferred_element_type=F32)
        if mode in ("scale", "scale_pad"):
            y = y * (HEAD_DIM ** -0.5 * LOG2E)
        elif mode == "sigmoid":
            y = 1.0 / (1.0 + jnp.exp(-y))
        elif mode == "kv_group":
            pos = pl.program_id(1) * rows + lax.broadcasted_iota(jnp.int32, y.shape, 0)
            col = lax.broadcasted_iota(jnp.int32, y.shape, 1)
            blk = col // LANES
            upper = col % LANES - HEAD_DIM
            hot = (blk == 0) & (upper == lax.shift_right_logical(pos, int(math.log2(SLC_LEN))))
            ones = ((blk == 1) | (blk == 3)) & (upper >= 0)
            y = y + jnp.where(hot | ones, 1.0, 0.0)
        y = y.astype(o_refs[oi].dtype)
        if mode == "scale_pad":
            zeros = jnp.zeros((y.shape[0], HEAD_DIM), y.dtype)
            pieces = []
            for c in range(0, width, HEAD_DIM):
                pieces += [y[:, c:c + HEAD_DIM], zeros]
            y = jnp.concatenate(pieces, axis=1)
        o_refs[oi][0, :, o0:o0 + y.shape[1]] = y


def _proj(x, g, shift, scale, weights, outs, plan, name):
    B, T, D = x.shape
    rows = min(PROJ_ROWS, T)
    in_specs = [
        pl.BlockSpec((1, rows, D), lambda b, i: (b, i, 0)),
        _const_spec((1, D)),
        pl.BlockSpec((1, 1, D), lambda b, i: (b, 0, 0)),
        pl.BlockSpec((1, 1, D), lambda b, i: (b, 0, 0)),
    ] + [_const_spec(w.shape) for w in weights]
    out_specs = [pl.BlockSpec((1, rows, n), lambda b, i: (b, i, 0)) for n, _ in outs]
    out_shape = [jax.ShapeDtypeStruct((B, T, n), dt) for n, dt in outs]
    res = pl.pallas_call(
        functools.partial(_proj_kernel, n_w=len(weights), plan=tuple(plan), rows=rows),
        grid=(B, T // rows),
        in_specs=in_specs,
        out_specs=out_specs,
        out_shape=out_shape,
        compiler_params=_cparams("arbitrary", "arbitrary"),
        name=name,
    )(x, g.reshape(1, D), shift.reshape(B, 1, D), scale.reshape(B, 1, D), *weights)
    return res


def _chunks(n, step=512):
    return [(c, min(step, n - c)) for c in range(0, n, step)]


def _mix_mlp_kernel(x_ref, o_ref, wo_ref, ga_ref, g_ref, sh_ref, sc_ref, gm_ref, w1_ref, w2_ref, fg_ref, y_ref,
                    *, d_ff, final):
    x = x_ref[0] + ga_ref[0] * jnp.dot(o_ref[0], wo_ref[...], preferred_element_type=F32)
    h = _rms_mod(x, g_ref[...], sh_ref[0], sc_ref[0]).astype(BF16)
    acc = jnp.zeros(x.shape, F32)
    for c0 in range(0, d_ff, FF_CHUNK):
        a = jnp.dot(h, w1_ref[:, c0:c0 + FF_CHUNK], preferred_element_type=F32)
        a = jnp.square(jnp.maximum(a, 0.0)).astype(BF16)
        acc = acc + jnp.dot(a, w2_ref[c0:c0 + FF_CHUNK, :], preferred_element_type=F32)
    y = x + gm_ref[0] * acc
    if final:
        ms = jnp.mean(y * y, axis=-1, keepdims=True)
        y = y * lax.rsqrt(ms + NORM_EPS) * fg_ref[...]
    y_ref[0] = y


def _mix_mlp_residual(x, o, w_out, gate_a, g, shift, scale, gate_m, w1, w2, final_gain, final):
    B, T, D = x.shape
    d_ff = w1.shape[1]
    rows = min(PROJ_ROWS, T)
    tile = pl.BlockSpec((1, rows, D), lambda b, i: (b, i, 0))
    vec = pl.BlockSpec((1, 1, D), lambda b, i: (b, 0, 0))
    per_batch = lambda v: v.reshape(B, 1, D)
    return pl.pallas_call(
        functools.partial(_mix_mlp_kernel, d_ff=d_ff, final=final),
        grid=(B, T // rows),
        in_specs=[tile, tile, _const_spec(w_out.shape), vec,
                  _const_spec((1, D)), vec, vec, vec,
                  _const_spec(w1.shape), _const_spec(w2.shape), _const_spec((1, D))],
        out_specs=tile,
        out_shape=jax.ShapeDtypeStruct((B, T, D), F32),
        compiler_params=_cparams("arbitrary", "arbitrary"),
        name="mix_mlp_residual",
    )(x, o, w_out, per_batch(gate_a), g.reshape(1, D), per_batch(shift), per_batch(scale), per_batch(gate_m),
      w1, w2, final_gain.reshape(1, D))


def _t5_bucket(dist):
    n = jnp.maximum(dist, 0)
    max_exact = REL_BUCKETS // 2
    nf = jnp.maximum(n, 1).astype(F32)
    large = max_exact + (jnp.log(nf / max_exact) / math.log(REL_MAX_DIST / max_exact)
                         * (REL_BUCKETS - max_exact)).astype(jnp.int32)
    large = jnp.minimum(large, REL_BUCKETS - 1)
    return jnp.where(n < max_exact, n, large)


def _bias_tiles(rel_bias, rels, tq, tk, window):
    iq = np.arange(tq)[:, None]
    ik = np.arange(tk)[None, :]
    dist = jnp.asarray(np.stack([r + iq - ik for r in rels]).astype(np.int32))
    hot = (_t5_bucket(dist)[..., None] == jnp.arange(REL_BUCKETS)).astype(F32)
    b = jnp.einsum("ntkb,bh->hntk", hot, rel_bias.astype(F32), precision=lax.Precision.HIGHEST)
    ok = dist >= 0
    if window is not None:
        ok = ok & (dist < window)
    tiles = jnp.where(ok[None], b * LOG2E, NEG)
    return jnp.concatenate([tiles, jnp.full_like(tiles[:, :1], NEG)], axis=1)


def _attend(q, tiles, stats):
    m, l, acc = stats
    for tile in tiles:
        k, v, bias = tile()
        s = lax.dot_general(q, k, (((1,), (1,)), ((), ())), preferred_element_type=F32)
        parts = [s[:, i:i + LANES] for i in range(0, s.shape[1], LANES)]
        if bias is not None:
            cols = [bias[:, i:i + LANES] for i in range(0, bias.shape[1], LANES)]
            parts = [t + cols[i % len(cols)] for i, t in enumerate(parts)]
        mx = functools.reduce(jnp.maximum, parts)
        m_new = jnp.maximum(m, jnp.max(mx, axis=-1, keepdims=True))
        alpha = jnp.exp2(m - m_new)
        p = [jnp.exp2(t - m_new) for t in parts]
        if l is not None:
            l = alpha * l + functools.reduce(jnp.add, p)
        pv = jnp.dot(jnp.concatenate(p, axis=1).astype(BF16), v, preferred_element_type=F32)
        acc = alpha * acc + pv
        m = m_new
    return m, l, acc


def _fresh_stats(rows, row_sums=True):
    zeros = jnp.zeros((rows, LANES), F32)
    return (jnp.full((rows, LANES), NEG, F32), zeros if row_sums else None, zeros)


def _causal_sweep(q, near, n_far, load_tile, cfar, scratch, side_work=None):
    m_sc, l_sc, acc_sc = scratch
    rows = q.shape[0]
    n_groups = lax.shift_right_logical(n_far, FAR_UNROLL_LOG2)
    rem = n_far - n_groups * FAR_UNROLL

    def save(m, l, acc):
        m_sc[...], acc_sc[...] = m, acc
        if l_sc is not None:
            l_sc[...] = l

    for r in range(FAR_UNROLL):
        @pl.when(rem == r)
        def _():
            extra = [functools.partial(load_tile, n_groups * FAR_UNROLL + i, cfar) for i in range(r)]
            m, l, acc = _attend(q, near + extra, _fresh_stats(rows, l_sc is not None))
            save(m - cfar(), l, acc)
            if side_work is not None:
                side_work()

    def group(g, carry):
        tiles = [functools.partial(load_tile, g * FAR_UNROLL + u, None) for u in range(FAR_UNROLL)]
        save(*_attend(q, tiles, (m_sc[...], None if l_sc is None else l_sc[...], acc_sc[...])))
        return carry

    lax.fori_loop(0, n_groups, group, 0)


def _diff_attn_kernel(q_ref, k_ref, v_ref, bias_ref, cfar_ref, lam_ref, subln_ref, o_ref,
                      m_sc, l_sc, acc_sc, *, tq, tk, lam_init):
    qi = pl.program_id(2)
    q = q_ref[0]
    lane = lax.broadcasted_iota(jnp.int32, q.shape, 1)
    zero = jnp.zeros_like(q)
    q2 = jnp.concatenate([jnp.where(lane < HEAD_DIM, q, zero),
                          jnp.where(lane >= HEAD_DIM, q, zero)], axis=0)

    def load_tile(j, bias):
        return (k_ref[0, pl.ds(j * tk, tk), :], v_ref[0, pl.ds(j * tk, tk), :], bias() if bias else None)

    ratio = tq // tk
    j0 = qi * ratio
    near = [functools.partial(load_tile, j0 + r, lambda r=r: bias_ref[0, r]) for r in range(ratio)]
    near.append(functools.partial(load_tile, jnp.maximum(j0 - 1, 0),
                                  lambda: bias_ref[0, jnp.where(qi >= 1, ratio, ratio + 1)]))
    _causal_sweep(q2, near, jnp.maximum(j0 - 1, 0), load_tile, lambda: cfar_ref[0], (m_sc, l_sc, acc_sc))

    lf = lam_ref[...]
    lam = (jnp.exp(jnp.sum(lf[0:1] * lf[1:2], axis=-1, keepdims=True))
           - jnp.exp(jnp.sum(lf[2:3] * lf[3:4], axis=-1, keepdims=True)) + lam_init)
    on = acc_sc[...] / jnp.sum(l_sc[...], axis=-1, keepdims=True)
    o = on[:tq] - lam * on[tq:]
    ms = jnp.mean(o * o, axis=-1, keepdims=True)
    o = o * lax.rsqrt(ms + NORM_EPS) * subln_ref[...] * (1.0 - lam_init)
    o_ref[0] = o.astype(o_ref.dtype)


def _diff_attention(qkv, rel_bias, lam, subln, layer_idx):
    B, T, D3 = qkv.shape
    D = D3 // 3
    H = DIFF_HEADS
    tq = min(DIFF_TQ, T)
    tk = min(DIFF_TK, tq)
    ratio = tq // tk
    lam_init = 0.8 - 0.6 * math.exp(-0.3 * layer_idx)
    assert 2 * tk - (tk - 1) >= REL_MAX_DIST
    rels = [-r * tk for r in range(ratio)] + [tk]
    nb = len(rels) + 1
    bt = _bias_tiles(rel_bias, rels, tq, tk, None)
    bias = bt.reshape(H, 2, nb, tq, tk).transpose(0, 2, 1, 3, 4).reshape(H, nb, 2 * tq, tk)
    cfar = jnp.repeat(rel_bias.astype(F32)[REL_BUCKETS - 1].reshape(H, 2) * LOG2E, tq, axis=1)
    cfar = jnp.broadcast_to(cfar.reshape(H, 2 * tq, 1), (H, 2 * tq, LANES))
    hb = D // LANES
    return pl.pallas_call(
        functools.partial(_diff_attn_kernel, tq=tq, tk=tk, lam_init=lam_init),
        grid=(B, H, T // tq),
        in_specs=[
            pl.BlockSpec((1, tq, LANES), lambda b, h, i: (b, i, h)),
            pl.BlockSpec((1, T, LANES), lambda b, h, i: (b, 0, hb + h)),
            pl.BlockSpec((1, T, LANES), lambda b, h, i: (b, 0, 2 * hb + h)),
            pl.BlockSpec((1, nb, 2 * tq, tk), lambda b, h, i: (h, 0, 0, 0)),
            pl.BlockSpec((1, 2 * tq, LANES), lambda b, h, i: (h, 0, 0)),
            _const_spec(lam.shape),
            _const_spec((1, LANES)),
        ],
        out_specs=pl.BlockSpec((1, tq, LANES), lambda b, h, i: (b, i, h)),
        out_shape=jax.ShapeDtypeStruct((B, T, D), BF16),
        scratch_shapes=[pltpu.VMEM((2 * tq, LANES), F32)] * 3,
        compiler_params=_cparams("arbitrary", "arbitrary", "arbitrary"),
        name="diff_attention",
    )(qkv, qkv, qkv, bias, cfar, lam.astype(F32), subln.astype(F32).reshape(1, LANES))


def _compress_kernel(a_ref, w1_ref, pos_ref, w2_ref, o_ref):
    half = CMP_STRIDE * HEAD_DIM
    for s in range(2):
        a = a_ref[s, 0, 0]
        w1 = w1_ref[s]
        p = jnp.dot(a, w1[:half], preferred_element_type=F32)
        q = jnp.dot(a, w1[half:], preferred_element_type=F32)
        posb = jnp.dot(jnp.broadcast_to(pos_ref[s], (8, 2 * half)).astype(BF16), w1,
                       preferred_element_type=F32)[0:1]
        hid = p + pltpu.roll(q, q.shape[0] - 1, axis=0) + posb
        hid = 0.5 * hid * (1.0 + jnp.tanh(math.sqrt(2.0 / math.pi) * (hid + 0.044715 * hid * hid * hid)))
        c = jnp.dot(hid.astype(BF16), w2_ref[s], preferred_element_type=F32)
        o_ref[0, 0, s] = jnp.concatenate([c, jnp.zeros_like(c)], axis=1).astype(o_ref.dtype)


def _compress(kv_cmp, cmp_pos, cmp_w1, cmp_w2):
    B, T, _ = kv_cmp.shape
    G, d = NSA_KV_HEADS, HEAD_DIM
    n = T // CMP_STRIDE
    a = kv_cmp.reshape(B, n, CMP_STRIDE, 2, G, d).transpose(3, 0, 4, 1, 2, 5).reshape(2, B, G, n, CMP_STRIDE * d)
    hid = cmp_w1.shape[-1]
    return pl.pallas_call(
        _compress_kernel,
        grid=(B, G),
        in_specs=[
            pl.BlockSpec((2, 1, 1, n, CMP_STRIDE * d), lambda b, g: (0, b, g, 0, 0)),
            _const_spec((2, CMP_LEN * d, hid)),
            _const_spec((2, 1, CMP_LEN * d)),
            _const_spec((2, hid, d)),
        ],
        out_specs=pl.BlockSpec((1, 1, 2, n, LANES), lambda b, g: (b, g, 0, 0, 0)),
        out_shape=jax.ShapeDtypeStruct((B, G, 2, n, LANES), BF16),
        compiler_params=_cparams("arbitrary", "arbitrary"),
        name="nsa_compress",
    )(a, cmp_w1.astype(BF16), cmp_pos.astype(F32).reshape(2, 1, CMP_LEN * d), cmp_w2.astype(BF16))


def _rows_from_lane_blocks(x):
    return jnp.concatenate([x[:, r * LANES:(r + 1) * LANES] for r in range(NSA_GROUP)], axis=0)


def _lane_blocks_from_rows(x):
    tq = x.shape[0] // NSA_GROUP
    return jnp.concatenate([x[r * tq:(r + 1) * tq] for r in range(NSA_GROUP)], axis=1)


def _nsa_cmp_kernel(q_ref, kc_ref, ov_ref, tri_ref, oc_ref, sel_ref, *, tq, top_k):
    oc, sel = _cmp_select(q_ref[0], pl.program_id(2) * tq, kc_ref[0, 0, 0], kc_ref[0, 0, 1],
                          ov_ref[...], tri_ref[...], top_k)
    oc_ref[0] = oc.astype(oc_ref.dtype)
    sel_ref[0, 0] = sel.astype(sel_ref.dtype)


def _cmp_select(q, q0, k_c, v_c, ov, tri, top_k):
    R = NSA_GROUP
    tq = q.shape[0]
    n_cmp = k_c.shape[0]
    q4 = _rows_from_lane_blocks(q)
    s = lax.dot_general(q4, k_c, (((1,), (1,)), ((), ())), preferred_element_type=F32)
    qpos = q0 + lax.broadcasted_iota(jnp.int32, (tq, 1), 0)
    cmp_end = lax.broadcasted_iota(jnp.int32, (1, n_cmp), 1) * CMP_STRIDE + (CMP_LEN - 1)
    valid4 = jnp.concatenate([cmp_end <= qpos] * R, axis=0)
    sm = jnp.where(valid4, s, NEG)
    e = jnp.exp2(sm - jnp.max(sm, axis=-1, keepdims=True))
    any_valid = jnp.concatenate([qpos >= CMP_LEN - 1] * R, axis=0)
    p = e * jnp.where(any_valid, 1.0 / jnp.sum(e, axis=-1, keepdims=True), 0.0)
    oc = _lane_blocks_from_rows(jnp.dot(p.astype(BF16), v_c, preferred_element_type=F32))

    psum = p[0:tq]
    for r in range(1, R):
        psum = psum + p[r * tq:(r + 1) * tq]
    hi = psum.astype(BF16)
    lo = (psum - hi.astype(F32)).astype(BF16)
    imp = jnp.dot(hi, ov, preferred_element_type=F32) + jnp.dot(lo, ov, preferred_element_type=F32)
    j = lax.broadcasted_iota(jnp.int32, (1, LANES), 1)
    qblk = lax.shift_right_logical(qpos, int(math.log2(SLC_LEN)))
    forced = (j == 0) | ((j <= qblk) & (j > qblk - SLC_FORCED_LOCAL))
    imp = jnp.where(forced, FORCE_BONUS, imp)
    imp = jnp.where(j <= qblk, imp, NEG)

    n_slc = tri.shape[0]
    vals = imp.T[:n_slc]
    rem = vals
    cnt = jnp.zeros((1, tq), F32)
    thr = jnp.full((1, tq), NEG, F32)
    for _ in range(top_k):
        mx = jnp.max(rem, axis=0, keepdims=True)
        active = cnt < top_k
        thr = jnp.where(active, mx, thr)
        hit = rem == mx
        cnt = cnt + jnp.where(active, jnp.sum(jnp.where(hit, 1.0, 0.0), axis=0, keepdims=True), 0.0)
        rem = jnp.where(hit, BELOW_NEG, rem)
    above = vals > thr
    n_above = jnp.sum(jnp.where(above, 1.0, 0.0), axis=0, keepdims=True)
    tie = vals == thr
    prefix = jnp.dot(tri, jnp.where(tie, 1.0, 0.0).astype(BF16), preferred_element_type=F32)
    chosen = (above | (tie & (prefix <= top_k - n_above))) & (vals > NEG)
    selb = jnp.where(chosen, 0.0, NEG)
    pieces = [jnp.zeros((HEAD_DIM, tq), F32), selb]
    if HEAD_DIM - n_slc:
        pieces.append(jnp.zeros((HEAD_DIM - n_slc, tq), F32))
    return oc, jnp.concatenate(pieces, axis=0).T


def _nsa_compressed_select(q, kc):
    B, T, _ = q.shape
    G, R = NSA_KV_HEADS, NSA_GROUP
    n_cmp = kc.shape[3]
    n_slc = T // SLC_LEN
    assert n_slc <= HEAD_DIM, "selection mask is carried in the 64 spare contraction lanes"
    top_k = min(SLC_TOPK, n_slc)
    tq = min(CMP_TQ, T)
    nn = np.arange(n_cmp)[:, None]
    jj = np.arange(LANES)[None, :]
    overlap = ((nn * CMP_STRIDE < jj * SLC_LEN + SLC_LEN) & (nn * CMP_STRIDE + CMP_LEN - 1 >= jj * SLC_LEN)
               & (jj < n_slc))
    overlap = overlap & (nn < (T - CMP_LEN) // CMP_STRIDE + 1)
    tri = np.tril(np.ones((n_slc, n_slc), np.float32))
    return pl.pallas_call(
        functools.partial(_nsa_cmp_kernel, tq=tq, top_k=top_k),
        grid=(B, G, T // tq),
        in_specs=[
            pl.BlockSpec((1, tq, R * LANES), lambda b, g, i: (b, i, g)),
            pl.BlockSpec((1, 1, 2, n_cmp, LANES), lambda b, g, i: (b, g, 0, 0, 0)),
            _const_spec((n_cmp, LANES)),
            _const_spec((n_slc, n_slc)),
        ],
        out_specs=[
            pl.BlockSpec((1, tq, R * LANES), lambda b, g, i: (b, i, g)),
            pl.BlockSpec((1, 1, tq, LANES), lambda b, g, i: (b, g, i, 0)),
        ],
        out_shape=[jax.ShapeDtypeStruct((B, T, G * R * LANES), BF16), jax.ShapeDtypeStruct((B, G, T, LANES), BF16)],
        compiler_params=_cparams("arbitrary", "arbitrary", "arbitrary"),
        name="nsa_compressed_select",
    )(q, kc, jnp.asarray(overlap.astype(np.float32), BF16), jnp.asarray(tri, BF16))


def _nsa_main_kernel(q_ref, sel_ref, ka_ref, vs_ref, kw_ref, vw_ref, bias_ref, cfar_ref, g_ref, ge_ref, cp_ref,
                     oc_ref, o_ref, ms_sc, as_sc, aw_sc, *, tq, tk, n_win):
    qi = pl.program_id(2)
    R = NSA_GROUP
    q4w = _rows_from_lane_blocks(q_ref[0])
    q4s = q4w + jnp.concatenate([sel_ref[0, 0]] * R, axis=0)

    def load_s(j, bias):
        return (ka_ref[0, pl.ds(j * tk, tk), :], vs_ref[0, pl.ds(j * tk, tk), :], bias() if bias else None)

    def load_w(j, bias):
        return (kw_ref[0, pl.ds(j * tk, tk), :], vw_ref[0, pl.ds(j * tk, tk), :], bias() if bias else None)

    near_s, near_w = [], []
    for t in range(n_win):
        j = jnp.maximum(qi - t, 0)
        b = (lambda: bias_ref[0, 0]) if t == 0 else (lambda t=t: bias_ref[0, jnp.where(qi >= t, t, n_win)])
        near_w.append(functools.partial(load_w, j, b))
        if t <= 1:
            near_s.append(functools.partial(load_s, j, b))

    def window_branch():
        _, _, aw_sc[...] = _attend(q4w, near_w, _fresh_stats(R * tq, row_sums=False))

    _causal_sweep(q4s, near_s, jnp.maximum(qi - 1, 0), load_s, lambda: cfar_ref[0], (ms_sc, None, as_sc),
                  side_work=window_branch)

    def normalised(acc):
        return acc / jnp.max(acc[:, HEAD_DIM:], axis=-1, keepdims=True)

    o_s = _lane_blocks_from_rows(normalised(as_sc[...]))
    o_w = _lane_blocks_from_rows(normalised(aw_sc[...]))
    g = g_ref[0]
    lane = lax.broadcasted_iota(jnp.int32, g.shape, 1)
    hi = g.astype(BF16)
    r1 = g - hi.astype(F32)
    mid = r1.astype(BF16)
    lo = (r1 - mid.astype(F32)).astype(BF16)
    n_g = 3 * R
    g_split = jnp.where(lane < n_g, hi, jnp.where(lane < 2 * n_g, mid, lo))
    ge = jnp.dot(g_split, ge_ref[...], preferred_element_type=F32)
    w = R * LANES
    o = ge[:, :w] * oc_ref[0].astype(F32) + ge[:, w:2 * w] * o_s + ge[:, 2 * w:] * o_w
    o_ref[0] = jnp.dot(o.astype(BF16), cp_ref[...], preferred_element_type=F32).astype(o_ref.dtype)


def _nsa_main(q, sel, kv, gates, o_cmp, rel_bias):
    B, T, _ = q.shape
    G, R = NSA_KV_HEADS, NSA_GROUP
    D = G * R * HEAD_DIM
    tq = min(NSA_TQ, T)
    tk = tq
    n_win = (WINDOW - 2) // tk + 2
    rels = [t * tk for t in range(n_win)]
    assert tk + tq - 1 < WINDOW and 2 * tk - (tk - 1) >= REL_MAX_DIST
    bt = _bias_tiles(rel_bias, rels, tq, tk, WINDOW)
    nb = n_win + 1
    bias = bt.reshape(G, R, nb, tq, tk).transpose(0, 2, 1, 3, 4).reshape(G, nb, R * tq, tk)
    cfar = jnp.repeat(rel_bias.astype(F32)[REL_BUCKETS - 1].reshape(G, R) * LOG2E, tq, axis=1)
    cfar = jnp.broadcast_to(cfar.reshape(G, R * tq, 1), (G, R * tq, LANES))
    ge = np.zeros((LANES, 3 * R * LANES), np.float32)
    for rep in range(3):
        for r in range(R):
            for br in range(3):
                c0 = br * R * LANES + r * LANES
                ge[rep * 3 * R + 3 * r + br, c0:c0 + HEAD_DIM] = 1.0
    cp = np.zeros((R * LANES, R * HEAD_DIM), np.float32)
    for r in range(R):
        cp[r * LANES + np.arange(HEAD_DIM), r * HEAD_DIM + np.arange(HEAD_DIM)] = 1.0
    blk = lambda k: pl.BlockSpec((1, T, LANES), lambda b, g, i: (b, 0, 4 * g + k))
    return pl.pallas_call(
        functools.partial(_nsa_main_kernel, tq=tq, tk=tk, n_win=n_win),
        grid=(B, G, T // tq),
        in_specs=[
            pl.BlockSpec((1, tq, R * LANES), lambda b, g, i: (b, i, g)),
            pl.BlockSpec((1, 1, tq, LANES), lambda b, g, i: (b, g, i, 0)),
            blk(0), blk(1), blk(2), blk(3),
            pl.BlockSpec((1, nb, R * tq, tk), lambda b, g, i: (g, 0, 0, 0)),
            pl.BlockSpec((1, R * tq, LANES), lambda b, g, i: (g, 0, 0)),
            pl.BlockSpec((1, tq, LANES), lambda b, g, i: (b, i, g)),
            _const_spec(ge.shape),
            _const_spec(cp.shape),
            pl.BlockSpec((1, tq, R * LANES), lambda b, g, i: (b, i, g)),
        ],
        out_specs=pl.BlockSpec((1, tq, R * HEAD_DIM), lambda b, g, i: (b, i, g)),
        out_shape=jax.ShapeDtypeStruct((B, T, D), BF16),
        scratch_shapes=[pltpu.VMEM((R * tq, LANES), F32)] * 3,
        compiler_params=_cparams("arbitrary", "arbitrary", "arbitrary"),
        name="nsa_main",
    )(q, sel, kv, kv, kv, kv, bias, cfar, gates, jnp.asarray(ge, BF16), jnp.asarray(cp, BF16), o_cmp)


def kernel(x, c, rel_bias, ada_w, ada_b, attn_norm, mlp_norm, mlp_w1, mlp_w2, a_w_in, a_w_out, a_lambda, a_subln, kv_ada_w, kv_ada_b, kv_norm, w_kv, cmp_pos, cmp_w1, cmp_w2, b_w_in, b_w_out, final_norm):
    B, T, D = x.shape
    depth = ada_w.shape[0]
    n_a = a_w_in.shape[0]
    G, R, d = NSA_KV_HEADS, NSA_GROUP, HEAD_DIM

    mod = _adaln(c, ada_w, ada_b)
    kv_mod = _adaln(c, kv_ada_w[None], kv_ada_b[None])[0]

    wkv = w_kv.reshape(D, 6, G, d)
    z = jnp.zeros((D, G, d), w_kv.dtype)
    w_cmp = wkv[:, 0:2].reshape(D, 2 * G * d).astype(BF16)
    w_sw = jnp.stack([wkv[:, 2], z, wkv[:, 3], z, wkv[:, 4], wkv[:, 5], wkv[:, 5], z], axis=2)
    w_sw = w_sw.reshape(D, G * 4 * LANES).astype(BF16)
    kv_plan = [(0, c0, w, 0, c0, "plain") for c0, w in _chunks(2 * G * d)]
    kv_plan += [(1, c0, w, 1, c0, "kv_group") for c0, w in _chunks(G * 4 * LANES, 4 * LANES)]

    shared = None
    for layer in range(depth):
        sh_a, sc_a, gt_a, sh_m, sc_m, gt_m = jnp.split(mod[layer], 6, axis=-1)
        if layer < n_a:
            w_in = a_w_in[layer].astype(BF16)
            plan = [(0, c0, w, 0, c0, "scale" if c0 < D else "plain") for c0, w in _chunks(3 * D)]
            (qkv,) = _proj(x, attn_norm[layer], sh_a, sc_a, [w_in], [(3 * D, BF16)], plan, "diff_qkv_proj")
            mix = _diff_attention(qkv, rel_bias, a_lambda[layer], a_subln[layer], layer)
            w_out = a_w_out[layer].astype(BF16)
        else:
            i = layer - n_a
            w_q = b_w_in[i][:, :D].astype(BF16)
            w_g = jnp.tile(b_w_in[i][:, D:].reshape(D, G, 3 * R), (1, 1, 3))
            w_g = jnp.pad(w_g, ((0, 0), (0, 0), (0, LANES - 9 * R))).reshape(D, G * LANES).astype(BF16)
            plan = [(0, c0, w, 0, 2 * c0, "scale_pad") for c0, w in _chunks(D)]
            plan += [(1, 0, G * LANES, 1, 0, "sigmoid")]
            q, gates = _proj(x, attn_norm[layer], sh_a, sc_a, [w_q, w_g], [(2 * D, BF16), (G * LANES, F32)], plan,
                             "nsa_in_proj")
            kc, kv_sw = shared
            o_cmp, sel = _nsa_compressed_select(q, kc)
            mix = _nsa_main(q, sel, kv_sw, gates, o_cmp, rel_bias)
            w_out = b_w_out[i].astype(BF16)
        x = _mix_mlp_residual(x, mix, w_out, gt_a, mlp_norm[layer], sh_m, sc_m, gt_m, mlp_w1[layer].astype(BF16),
                              mlp_w2[layer].astype(BF16), final_norm, final=(layer == depth - 1))
        if layer == n_a - 1:
            sh_kv, sc_kv = jnp.split(kv_mod, 2, axis=-1)
            kv_cmp, kv_sw = _proj(x, kv_norm, sh_kv, sc_kv, [w_cmp, w_sw], [(2 * G * d, BF16), (G * 4 * LANES, BF16)],
                                  kv_plan, "nsa_kv_proj")
            shared = (_compress(kv_cmp, cmp_pos, cmp_w1, cmp_w2), kv_sw)
    return x
```

```python
import functools
import math

import numpy as np
import jax
import jax.numpy as jnp
from jax import lax
from jax.experimental import pallas as pl
from jax.experimental.pallas import tpu as pltpu

F32 = jnp.float32
BF16 = jnp.bfloat16

NEG = -1e30
BELOW_NEG = -3e38
NORM_EPS = 1e-6

HEAD_DIM = 64
DIFF_HEADS = 8
NSA_KV_HEADS = 4
NSA_GROUP = 4
REL_BUCKETS = 32
REL_MAX_DIST = 128
CMP_LEN = 32
CMP_STRIDE = 16
SLC_LEN = 64
SLC_TOPK = 16
SLC_FORCED_LOCAL = 2
FORCE_BONUS = 1e4
WINDOW = 512

LANES = 128
VMEM_LIMIT_BYTES = 56 * 1024 * 1024

PROJ_ROWS = 512
DIFF_TQ = 512
DIFF_TK = 256
FAR_UNROLL_LOG2 = 4
FAR_UNROLL = 1 << FAR_UNROLL_LOG2
LOG2E = math.log2(math.e)
NSA_TQ = 256
CMP_TQ = 1024
FF_CHUNK = 1024


def _cparams(*sem):
    return pltpu.CompilerParams(dimension_semantics=sem, vmem_limit_bytes=VMEM_LIMIT_BYTES)


def _const_spec(shape):
    nd = len(shape)
    return pl.BlockSpec(shape, lambda *_: (0,) * nd, pipeline_mode=pl.Buffered(1))


def _adaln_kernel(c_ref, w_ref, b_ref, o_ref):
    c = c_ref[...]
    c_act = (c * (1.0 / (1.0 + jnp.exp(-c)))).astype(BF16)
    o_ref[0] = jnp.dot(c_act, w_ref[0].astype(BF16), preferred_element_type=F32) + b_ref[0]


def _adaln(c, w, b, tn=2048):
    L, D, N = w.shape
    B = c.shape[0]
    tn = min(tn, N)
    return pl.pallas_call(
        _adaln_kernel,
        grid=(L, N // tn),
        in_specs=[
            pl.BlockSpec((B, D), lambda l, j: (0, 0)),
            pl.BlockSpec((1, D, tn), lambda l, j: (l, 0, j)),
            pl.BlockSpec((1, 1, tn), lambda l, j: (l, 0, j)),
        ],
        out_specs=pl.BlockSpec((1, B, tn), lambda l, j: (l, 0, j)),
        out_shape=jax.ShapeDtypeStruct((L, B, N), F32),
        compiler_params=_cparams("arbitrary", "arbitrary"),
        name="adaln",
    )(c, w, b.reshape(L, 1, N))


def _rms_mod(x, g, shift, scale):
    ms = jnp.mean(x * x, axis=-1, keepdims=True)
    y = x * lax.rsqrt(ms + NORM_EPS) * g
    return y * (1.0 + scale) + shift


def _proj_kernel(x_ref, g_ref, sh_ref, sc_ref, *refs, n_w, plan, rows):
    w_refs, o_refs = refs[:n_w], refs[n_w:]
    h = _rms_mod(x_ref[0], g_ref[...], sh_ref[0], sc_ref[0]).astype(BF16)
    for (wi, c0, width, oi, o0, mode) in plan:
        y = jnp.dot(h, w_refs[wi][:, c0:c0 + width], preferred_element_type=F32)
        if mode in ("scale", "scale_pad"):
            y = y * (HEAD_DIM ** -0.5 * LOG2E)
        elif mode == "sigmoid":
            y = 1.0 / (1.0 + jnp.exp(-y))
        elif mode == "kv_group":
            pos = pl.program_id(1) * rows + lax.broadcasted_iota(jnp.int32, y.shape, 0)
            col = lax.broadcasted_iota(jnp.int32, y.shape, 1)
            blk = col // LANES
            upper = col % LANES - HEAD_DIM
            hot = (blk == 0) & (upper == lax.shift_right_logical(pos, int(math.log2(SLC_LEN))))
            ones = ((blk == 1) | (blk == 3)) & (upper >= 0)
            y = y + jnp.where(hot | ones, 1.0, 0.0)
        y = y.astype(o_refs[oi].dtype)
        if mode == "scale_pad":
            zeros = jnp.zeros((y.shape[0], HEAD_DIM), y.dtype)
            pieces = []
            for c in range(0, width, HEAD_DIM):
                pieces += [y[:, c:c + HEAD_DIM], zeros]
            y = jnp.concatenate(pieces, axis=1)
        o_refs[oi][0, :, o0:o0 + y.shape[1]] = y


def _proj(x, g, shift, scale, weights, outs, plan, name):
    B, T, D = x.shape
    rows = min(PROJ_ROWS, T)
    in_specs = [
        pl.BlockSpec((1, rows, D), lambda b, i: (b, i, 0)),
        _const_spec((1, D)),
        pl.BlockSpec((1, 1, D), lambda b, i: (b, 0, 0)),
        pl.BlockSpec((1, 1, D), lambda b, i: (b, 0, 0)),
    ] + [_const_spec(w.shape) for w in weights]
    out_specs = [pl.BlockSpec((1, rows, n), lambda b, i: (b, i, 0)) for n, _ in outs]
    out_shape = [jax.ShapeDtypeStruct((B, T, n), dt) for n, dt in outs]
    res = pl.pallas_call(
        functools.partial(_proj_kernel, n_w=len(weights), plan=tuple(plan), rows=rows),
        grid=(B, T // rows),
        in_specs=in_specs,
        out_specs=out_specs,
        out_shape=out_shape,
        compiler_params=_cparams("arbitrary", "arbitrary"),
        name=name,
    )(x, g.reshape(1, D), shift.reshape(B, 1, D), scale.reshape(B, 1, D), *weights)
    return res


def _chunks(n, step=512):
    return [(c, min(step, n - c)) for c in range(0, n, step)]


def _mix_mlp_kernel(x_ref, o_ref, wo_ref, ga_ref, g_ref, sh_ref, sc_ref, gm_ref, w1_ref, w2_ref, fg_ref, y_ref,
                    *, d_ff, final):
    x = x_ref[0] + ga_ref[0] * jnp.dot(o_ref[0], wo_ref[...], preferred_element_type=F32)
    h = _rms_mod(x, g_ref[...], sh_ref[0], sc_ref[0]).astype(BF16)
    acc = jnp.zeros(x.shape, F32)
    for c0 in range(0, d_ff, FF_CHUNK):
        a = jnp.dot(h, w1_ref[:, c0:c0 + FF_CHUNK], preferred_element_type=F32)
        a = jnp.square(jnp.maximum(a, 0.0)).astype(BF16)
        acc = acc + jnp.dot(a, w2_ref[c0:c0 + FF_CHUNK, :], preferred_element_type=F32)
    y = x + gm_ref[0] * acc
    if final:
        ms = jnp.mean(y * y, axis=-1, keepdims=True)
        y = y * lax.rsqrt(ms + NORM_EPS) * fg_ref[...]
    y_ref[0] = y


def _mix_mlp_residual(x, o, w_out, gate_a, g, shift, scale, gate_m, w1, w2, final_gain, final):
    B, T, D = x.shape
    d_ff = w1.shape[1]
    rows = min(PROJ_ROWS, T)
    tile = pl.BlockSpec((1, rows, D), lambda b, i: (b, i, 0))
    vec = pl.BlockSpec((1, 1, D), lambda b, i: (b, 0, 0))
    per_batch = lambda v: v.reshape(B, 1, D)
    return pl.pallas_call(
        functools.partial(_mix_mlp_kernel, d_ff=d_ff, final=final),
        grid=(B, T // rows),
        in_specs=[tile, tile, _const_spec(w_out.shape), vec,
                  _const_spec((1, D)), vec, vec, vec,
                  _const_spec(w1.shape), _const_spec(w2.shape), _const_spec((1, D))],
        out_specs=tile,
        out_shape=jax.ShapeDtypeStruct((B, T, D), F32),
        compiler_params=_cparams("arbitrary", "arbitrary"),
        name="mix_mlp_residual",
    )(x, o, w_out, per_batch(gate_a), g.reshape(1, D), per_batch(shift), per_batch(scale), per_batch(gate_m),
      w1, w2, final_gain.reshape(1, D))


def _t5_bucket(dist):
    n = jnp.maximum(dist, 0)
    max_exact = REL_BUCKETS // 2
    nf = jnp.maximum(n, 1).astype(F32)
    large = max_exact + (jnp.log(nf / max_exact) / math.log(REL_MAX_DIST / max_exact)
                         * (REL_BUCKETS - max_exact)).astype(jnp.int32)
    large = jnp.minimum(large, REL_BUCKETS - 1)
    return jnp.where(n < max_exact, n, large)


def _bias_tiles(rel_bias, rels, tq, tk, window):
    iq = np.arange(tq)[:, None]
    ik = np.arange(tk)[None, :]
    dist = jnp.asarray(np.stack([r + iq - ik for r in rels]).astype(np.int32))
    hot = (_t5_bucket(dist)[..., None] == jnp.arange(REL_BUCKETS)).astype(F32)
    b = jnp.einsum("ntkb,bh->hntk", hot, rel_bias.astype(F32), precision=lax.Precision.HIGHEST)
    ok = dist >= 0
    if window is not None:
        ok = ok & (dist < window)
    tiles = jnp.where(ok[None], b * LOG2E, NEG)
    return jnp.concatenate([tiles, jnp.full_like(tiles[:, :1], NEG)], axis=1)


def _attend(q, tiles, stats):
    m, l, acc = stats
    for tile in tiles:
        k, v, bias = tile()
        s = lax.dot_general(q, k, (((1,), (1,)), ((), ())), preferred_element_type=F32)
        parts = [s[:, i:i + LANES] for i in range(0, s.shape[1], LANES)]
        if bias is not None:
            cols = [bias[:, i:i + LANES] for i in range(0, bias.shape[1], LANES)]
            parts = [t + cols[i % len(cols)] for i, t in enumerate(parts)]
        mx = functools.reduce(jnp.maximum, parts)
        m_new = jnp.maximum(m, jnp.max(mx, axis=-1, keepdims=True))
        alpha = jnp.exp2(m - m_new)
        p = [jnp.exp2(t - m_new) for t in parts]
        if l is not None:
            l = alpha * l + functools.reduce(jnp.add, p)
        pv = jnp.dot(jnp.concatenate(p, axis=1).astype(BF16), v, preferred_element_type=F32)
        acc = alpha * acc + pv
        m = m_new
    return m, l, acc


def _fresh_stats(rows, row_sums=True):
    zeros = jnp.zeros((rows, LANES), F32)
    return (jnp.full((rows, LANES), NEG, F32), zeros if row_sums else None, zeros)


def _causal_sweep(q, near, n_far, load_tile, cfar, scratch, side_work=None):
    m_sc, l_sc, acc_sc = scratch
    rows = q.shape[0]
    n_groups = lax.shift_right_logical(n_far, FAR_UNROLL_LOG2)
    rem = n_far - n_groups * FAR_UNROLL

    def save(m, l, acc):
        m_sc[...], acc_sc[...] = m, acc
        if l_sc is not None:
            l_sc[...] = l

    for r in range(FAR_UNROLL):
        @pl.when(rem == r)
        def _():
            extra = [functools.partial(load_tile, n_groups * FAR_UNROLL + i, cfar) for i in range(r)]
            m, l, acc = _attend(q, near + extra, _fresh_stats(rows, l_sc is not None))
            save(m - cfar(), l, acc)
            if side_work is not None:
                side_work()

    def group(g, carry):
        tiles = [functools.partial(load_tile, g * FAR_UNROLL + u, None) for u in range(FAR_UNROLL)]
        save(*_attend(q, tiles, (m_sc[...], None if l_sc is None else l_sc[...], acc_sc[...])))
        return carry

    lax.fori_loop(0, n_groups, group, 0)


def _diff_attn_kernel(q_ref, k_ref, v_ref, bias_ref, cfar_ref, lam_ref, subln_ref, o_ref,
                      m_sc, l_sc, acc_sc, *, tq, tk, lam_init):
    qi = pl.program_id(2)
    q = q_ref[0]
    lane = lax.broadcasted_iota(jnp.int32, q.shape, 1)
    zero = jnp.zeros_like(q)
    q2 = jnp.concatenate([jnp.where(lane < HEAD_DIM, q, zero),
                          jnp.where(lane >= HEAD_DIM, q, zero)], axis=0)

    def load_tile(j, bias):
        return (k_ref[0, pl.ds(j * tk, tk), :], v_ref[0, pl.ds(j * tk, tk), :], bias() if bias else None)

    ratio = tq // tk
    j0 = qi * ratio
    near = [functools.partial(load_tile, j0 + r, lambda r=r: bias_ref[0, r]) for r in range(ratio)]
    near.append(functools.partial(load_tile, jnp.maximum(j0 - 1, 0),
                                  lambda: bias_ref[0, jnp.where(qi >= 1, ratio, ratio + 1)]))
    _causal_sweep(q2, near, jnp.maximum(j0 - 1, 0), load_tile, lambda: cfar_ref[0], (m_sc, l_sc, acc_sc))

    lf = lam_ref[...]
    lam = (jnp.exp(jnp.sum(lf[0:1] * lf[1:2], axis=-1, keepdims=True))
           - jnp.exp(jnp.sum(lf[2:3] * lf[3:4], axis=-1, keepdims=True)) + lam_init)
    on = acc_sc[...] / jnp.sum(l_sc[...], axis=-1, keepdims=True)
    o = on[:tq] - lam * on[tq:]
    ms = jnp.mean(o * o, axis=-1, keepdims=True)
    o = o * lax.rsqrt(ms + NORM_EPS) * subln_ref[...] * (1.0 - lam_init)
    o_ref[0] = o.astype(o_ref.dtype)


def _diff_attention(qkv, rel_bias, lam, subln, layer_idx):
    B, T, D3 = qkv.shape
    D = D3 // 3
    H = DIFF_HEADS
    tq = min(DIFF_TQ, T)
    tk = min(DIFF_TK, tq)
    ratio = tq // tk
    lam_init = 0.8 - 0.6 * math.exp(-0.3 * layer_idx)
    assert 2 * tk - (tk - 1) >= REL_MAX_DIST
    rels = [-r * tk for r in range(ratio)] + [tk]
    nb = len(rels) + 1
    bt = _bias_tiles(rel_bias, rels, tq, tk, None)
    bias = bt.reshape(H, 2, nb, tq, tk).transpose(0, 2, 1, 3, 4).reshape(H, nb, 2 * tq, tk)
    cfar = jnp.repeat(rel_bias.astype(F32)[REL_BUCKETS - 1].reshape(H, 2) * LOG2E, tq, axis=1)
    cfar = jnp.broadcast_to(cfar.reshape(H, 2 * tq, 1), (H, 2 * tq, LANES))
    hb = D // LANES
    return pl.pallas_call(
        functools.partial(_diff_attn_kernel, tq=tq, tk=tk, lam_init=lam_init),
        grid=(B, H, T // tq),
        in_specs=[
            pl.BlockSpec((1, tq, LANES), lambda b, h, i: (b, i, h)),
            pl.BlockSpec((1, T, LANES), lambda b, h, i: (b, 0, hb + h)),
            pl.BlockSpec((1, T, LANES), lambda b, h, i: (b, 0, 2 * hb + h)),
            pl.BlockSpec((1, nb, 2 * tq, tk), lambda b, h, i: (h, 0, 0, 0)),
            pl.BlockSpec((1, 2 * tq, LANES), lambda b, h, i: (h, 0, 0)),
            _const_spec(lam.shape),
            _const_spec((1, LANES)),
        ],
        out_specs=pl.BlockSpec((1, tq, LANES), lambda b, h, i: (b, i, h)),
        out_shape=jax.ShapeDtypeStruct((B, T, D), BF16),
        scratch_shapes=[pltpu.VMEM((2 * tq, LANES), F32)] * 3,
        compiler_params=_cparams("arbitrary", "arbitrary", "arbitrary"),
        name="diff_attention",
    )(qkv, qkv, qkv, bias, cfar, lam.astype(F32), subln.astype(F32).reshape(1, LANES))


def _compress_kernel(a_ref, w1_ref, pos_ref, w2_ref, o_ref):
    half = CMP_STRIDE * HEAD_DIM
    for s in range(2):
        a = a_ref[s, 0, 0]
        w1 = w1_ref[s]
        p = jnp.dot(a, w1[:half], preferred_element_type=F32)
        q = jnp.dot(a, w1[half:], preferred_element_type=F32)
        posb = jnp.dot(jnp.broadcast_to(pos_ref[s], (8, 2 * half)).astype(BF16), w1,
                       preferred_element_type=F32)[0:1]
        hid = p + pltpu.roll(q, q.shape[0] - 1, axis=0) + posb
        hid = 0.5 * hid * (1.0 + jnp.tanh(math.sqrt(2.0 / math.pi) * (hid + 0.044715 * hid * hid * hid)))
        c = jnp.dot(hid.astype(BF16), w2_ref[s], preferred_element_type=F32)
        o_ref[0, 0, s] = jnp.concatenate([c, jnp.zeros_like(c)], axis=1).astype(o_ref.dtype)


def _compress(kv_cmp, cmp_pos, cmp_w1, cmp_w2):
    B, T, _ = kv_cmp.shape
    G, d = NSA_KV_HEADS, HEAD_DIM
    n = T // CMP_STRIDE
    a = kv_cmp.reshape(B, n, CMP_STRIDE, 2, G, d).transpose(3, 0, 4, 1, 2, 5).reshape(2, B, G, n, CMP_STRIDE * d)
    hid = cmp_w1.shape[-1]
    return pl.pallas_call(
        _compress_kernel,
        grid=(B, G),
        in_specs=[
            pl.BlockSpec((2, 1, 1, n, CMP_STRIDE * d), lambda b, g: (0, b, g, 0, 0)),
            _const_spec((2, CMP_LEN * d, hid)),
            _const_spec((2, 1, CMP_LEN * d)),
            _const_spec((2, hid, d)),
        ],
        out_specs=pl.BlockSpec((1, 1, 2, n, LANES), lambda b, g: (b, g, 0, 0, 0)),
        out_shape=jax.ShapeDtypeStruct((B, G, 2, n, LANES), BF16),
        compiler_params=_cparams("arbitrary", "arbitrary"),
        name="nsa_compress",
    )(a, cmp_w1.astype(BF16), cmp_pos.astype(F32).reshape(2, 1, CMP_LEN * d), cmp_w2.astype(BF16))


def _rows_from_lane_blocks(x):
    return jnp.concatenate([x[:, r * LANES:(r + 1) * LANES] for r in range(NSA_GROUP)], axis=0)


def _lane_blocks_from_rows(x):
    tq = x.shape[0] // NSA_GROUP
    return jnp.concatenate([x[r * tq:(r + 1) * tq] for r in range(NSA_GROUP)], axis=1)


def _nsa_cmp_kernel(q_ref, kc_ref, ov_ref, tri_ref, oc_ref, sel_ref, *, tq, top_k):
    oc, sel = _cmp_select(q_ref[0], pl.program_id(2) * tq, kc_ref[0, 0, 0], kc_ref[0, 0, 1],
                          ov_ref[...], tri_ref[...], top_k)
    oc_ref[0] = oc.astype(oc_ref.dtype)
    sel_ref[0, 0] = sel.astype(sel_ref.dtype)


def _cmp_select(q, q0, k_c, v_c, ov, tri, top_k):
    R = NSA_GROUP
    tq = q.shape[0]
    n_cmp = k_c.shape[0]
    q4 = _rows_from_lane_blocks(q)
    s = lax.dot_general(q4, k_c, (((1,), (1,)), ((), ())), preferred_element_type=F32)
    qpos = q0 + lax.broadcasted_iota(jnp.int32, (tq, 1), 0)
    cmp_end = lax.broadcasted_iota(jnp.int32, (1, n_cmp), 1) * CMP_STRIDE + (CMP_LEN - 1)
    valid4 = jnp.concatenate([cmp_end <= qpos] * R, axis=0)
    sm = jnp.where(valid4, s, NEG)
    e = jnp.exp2(sm - jnp.max(sm, axis=-1, keepdims=True))
    any_valid = jnp.concatenate([qpos >= CMP_LEN - 1] * R, axis=0)
    p = e * jnp.where(any_valid, 1.0 / jnp.sum(e, axis=-1, keepdims=True), 0.0)
    oc = _lane_blocks_from_rows(jnp.dot(p.astype(BF16), v_c, preferred_element_type=F32))

    psum = p[0:tq]
    for r in range(1, R):
        psum = psum + p[r * tq:(r + 1) * tq]
    hi = psum.astype(BF16)
    lo = (psum - hi.astype(F32)).astype(BF16)
    imp = jnp.dot(hi, ov, preferred_element_type=F32) + jnp.dot(lo, ov, preferred_element_type=F32)
    j = lax.broadcasted_iota(jnp.int32, (1, LANES), 1)
    qblk = lax.shift_right_logical(qpos, int(math.log2(SLC_LEN)))
    forced = (j == 0) | ((j <= qblk) & (j > qblk - SLC_FORCED_LOCAL))
    imp = jnp.where(forced, FORCE_BONUS, imp)
    imp = jnp.where(j <= qblk, imp, NEG)

    n_slc = tri.shape[0]
    vals = imp.T[:n_slc]
    rem = vals
    cnt = jnp.zeros((1, tq), F32)
    thr = jnp.full((1, tq), NEG, F32)
    for _ in range(top_k):
        mx = jnp.max(rem, axis=0, keepdims=True)
        active = cnt < top_k
        thr = jnp.where(active, mx, thr)
        hit = rem == mx
        cnt = cnt + jnp.where(active, jnp.sum(jnp.where(hit, 1.0, 0.0), axis=0, keepdims=True), 0.0)
        rem = jnp.where(hit, BELOW_NEG, rem)
    above = vals > thr
    n_above = jnp.sum(jnp.where(above, 1.0, 0.0), axis=0, keepdims=True)
    tie = vals == thr
    prefix = jnp.dot(tri, jnp.where(tie, 1.0, 0.0).astype(BF16), preferred_element_type=F32)
    chosen = (above | (tie & (prefix <= top_k - n_above))) & (vals > NEG)
    selb = jnp.where(chosen, 0.0, NEG)
    pieces = [jnp.zeros((HEAD_DIM, tq), F32), selb]
    if HEAD_DIM - n_slc:
        pieces.append(jnp.zeros((HEAD_DIM - n_slc, tq), F32))
    return oc, jnp.concatenate(pieces, axis=0).T


def _nsa_compressed_select(q, kc):
    B, T, _ = q.shape
    G, R = NSA_KV_HEADS, NSA_GROUP
    n_cmp = kc.shape[3]
    n_slc = T // SLC_LEN
    assert n_slc <= HEAD_DIM, "selection mask is carried in the 64 spare contraction lanes"
    top_k = min(SLC_TOPK, n_slc)
    tq = min(CMP_TQ, T)
    nn = np.arange(n_cmp)[:, None]
    jj = np.arange(LANES)[None, :]
    overlap = ((nn * CMP_STRIDE < jj * SLC_LEN + SLC_LEN) & (nn * CMP_STRIDE + CMP_LEN - 1 >= jj * SLC_LEN)
               & (jj < n_slc))
    overlap = overlap & (nn < (T - CMP_LEN) // CMP_STRIDE + 1)
    tri = np.tril(np.ones((n_slc, n_slc), np.float32))
    return pl.pallas_call(
        functools.partial(_nsa_cmp_kernel, tq=tq, top_k=top_k),
        grid=(B, G, T // tq),
        in_specs=[
            pl.BlockSpec((1, tq, R * LANES), lambda b, g, i: (b, i, g)),
            pl.BlockSpec((1, 1, 2, n_cmp, LANES), lambda b, g, i: (b, g, 0, 0, 0)),
            _const_spec((n_cmp, LANES)),
            _const_spec((n_slc, n_slc)),
        ],
        out_specs=[
            pl.BlockSpec((1, tq, R * LANES), lambda b, g, i: (b, i, g)),
            pl.BlockSpec((1, 1, tq, LANES), lambda b, g, i: (b, g, i, 0)),
        ],
        out_shape=[jax.ShapeDtypeStruct((B, T, G * R * LANES), BF16), jax.ShapeDtypeStruct((B, G, T, LANES), BF16)],
        compiler_params=_cparams("arbitrary", "arbitrary", "arbitrary"),
        name="nsa_compressed_select",
    )(q, kc, jnp.asarray(overlap.astype(np.float32), BF16), jnp.asarray(tri, BF16))


def _nsa_main_kernel(q_ref, sel_ref, ka_ref, vs_ref, kw_ref, vw_ref, bias_ref, cfar_ref, g_ref, ge_ref, cp_ref,
                     oc_ref, o_ref, ms_sc, as_sc, aw_sc, *, tq, tk, n_win):
    qi = pl.program_id(2)
    R = NSA_GROUP
    q4w = _rows_from_lane_blocks(q_ref[0])
    q4s = q4w + jnp.concatenate([sel_ref[0, 0]] * R, axis=0)

    def load_s(j, bias):
        return (ka_ref[0, pl.ds(j * tk, tk), :], vs_ref[0, pl.ds(j * tk, tk), :], bias() if bias else None)

    def load_w(j, bias):
        return (kw_ref[0, pl.ds(j * tk, tk), :], vw_ref[0, pl.ds(j * tk, tk), :], bias() if bias else None)

    near_s, near_w = [], []
    for t in range(n_win):
        j = jnp.maximum(qi - t, 0)
        b = (lambda: bias_ref[0, 0]) if t == 0 else (lambda t=t: bias_ref[0, jnp.where(qi >= t, t, n_win)])
        near_w.append(functools.partial(load_w, j, b))
        if t <= 1:
            near_s.append(functools.partial(load_s, j, b))

    def window_branch():
        _, _, aw_sc[...] = _attend(q4w, near_w, _fresh_stats(R * tq, row_sums=False))

    _causal_sweep(q4s, near_s, jnp.maximum(qi - 1, 0), load_s, lambda: cfar_ref[0], (ms_sc, None, as_sc),
                  side_work=window_branch)

    def normalised(acc):
        return acc / jnp.max(acc[:, HEAD_DIM:], axis=-1, keepdims=True)

    o_s = _lane_blocks_from_rows(normalised(as_sc[...]))
    o_w = _lane_blocks_from_rows(normalised(aw_sc[...]))
    g = g_ref[0]
    lane = lax.broadcasted_iota(jnp.int32, g.shape, 1)
    hi = g.astype(BF16)
    r1 = g - hi.astype(F32)
    mid = r1.astype(BF16)
    lo = (r1 - mid.astype(F32)).astype(BF16)
    n_g = 3 * R
    g_split = jnp.where(lane < n_g, hi, jnp.where(lane < 2 * n_g, mid, lo))
    ge = jnp.dot(g_split, ge_ref[...], preferred_element_type=F32)
    w = R * LANES
    o = ge[:, :w] * oc_ref[0].astype(F32) + ge[:, w:2 * w] * o_s + ge[:, 2 * w:] * o_w
    o_ref[0] = jnp.dot(o.astype(BF16), cp_ref[...], preferred_element_type=F32).astype(o_ref.dtype)


def _nsa_main(q, sel, kv, gates, o_cmp, rel_bias):
    B, T, _ = q.shape
    G, R = NSA_KV_HEADS, NSA_GROUP
    D = G * R * HEAD_DIM
    tq = min(NSA_TQ, T)
    tk = tq
    n_win = (WINDOW - 2) // tk + 2
    rels = [t * tk for t in range(n_win)]
    assert tk + tq - 1 < WINDOW and 2 * tk - (tk - 1) >= REL_MAX_DIST
    bt = _bias_tiles(rel_bias, rels, tq, tk, WINDOW)
    nb = n_win + 1
    bias = bt.reshape(G, R, nb, tq, tk).transpose(0, 2, 1, 3, 4).reshape(G, nb, R * tq, tk)
    cfar = jnp.repeat(rel_bias.astype(F32)[REL_BUCKETS - 1].reshape(G, R) * LOG2E, tq, axis=1)
    cfar = jnp.broadcast_to(cfar.reshape(G, R * tq, 1), (G, R * tq, LANES))
    ge = np.zeros((LANES, 3 * R * LANES), np.float32)
    for rep in range(3):
        for r in range(R):
            for br in range(3):
                c0 = br * R * LANES + r * LANES
                ge[rep * 3 * R + 3 * r + br, c0:c0 + HEAD_DIM] = 1.0
    cp = np.zeros((R * LANES, R * HEAD_DIM), np.float32)
    for r in range(R):
        cp[r * LANES + np.arange(HEAD_DIM), r * HEAD_DIM + np.arange(HEAD_DIM)] = 1.0
    blk = lambda k: pl.BlockSpec((1, T, LANES), lambda b, g, i: (b, 0, 4 * g + k))
    return pl.pallas_call(
        functools.partial(_nsa_main_kernel, tq=tq, tk=tk, n_win=n_win),
        grid=(B, G, T // tq),
        in_specs=[
            pl.BlockSpec((1, tq, R * LANES), lambda b, g, i: (b, i, g)),
            pl.BlockSpec((1, 1, tq, LANES), lambda b, g, i: (b, g, i, 0)),
            blk(0), blk(1), blk(2), blk(3),
            pl.BlockSpec((1, nb, R * tq, tk), lambda b, g, i: (g, 0, 0, 0)),
            pl.BlockSpec((1, R * tq, LANES), lambda b, g, i: (g, 0, 0)),
            pl.BlockSpec((1, tq, LANES), lambda b, g, i: (b, i, g)),
            _const_spec(ge.shape),
            _const_spec(cp.shape),
            pl.BlockSpec((1, tq, R * LANES), lambda b, g, i: (b, i, g)),
        ],
        out_specs=pl.BlockSpec((1, tq, R * HEAD_DIM), lambda b, g, i: (b, i, g)),
        out_shape=jax.ShapeDtypeStruct((B, T, D), BF16),
        scratch_shapes=[pltpu.VMEM((R * tq, LANES), F32)] * 3,
        compiler_params=_cparams("arbitrary", "arbitrary", "arbitrary"),
        name="nsa_main",
    )(q, sel, kv, kv, kv, kv, bias, cfar, gates, jnp.asarray(ge, BF16), jnp.asarray(cp, BF16), o_cmp)


def kernel(x, c, rel_bias, ada_w, ada_b, attn_norm, mlp_norm, mlp_w1, mlp_w2, a_w_in, a_w_out, a_lambda, a_subln, kv_ada_w, kv_ada_b, kv_norm, w_kv, cmp_pos, cmp_w1, cmp_w2, b_w_in, b_w_out, final_norm):
    B, T, D = x.shape
    depth = ada_w.shape[0]
    n_a = a_w_in.shape[0]
    G, R, d = NSA_KV_HEADS, NSA_GROUP, HEAD_DIM

    mod = _adaln(c, ada_w, ada_b)
    kv_mod = _adaln(c, kv_ada_w[None], kv_ada_b[None])[0]

    wkv = w_kv.reshape(D, 6, G, d)
    z = jnp.zeros((D, G, d), w_kv.dtype)
    w_cmp = wkv[:, 0:2].reshape(D, 2 * G * d).astype(BF16)
    w_sw = jnp.stack([wkv[:, 2], z, wkv[:, 3], z, wkv[:, 4], wkv[:, 5], wkv[:, 5], z], axis=2)
    w_sw = w_sw.reshape(D, G * 4 * LANES).astype(BF16)
    kv_plan = [(0, c0, w, 0, c0, "plain") for c0, w in _chunks(2 * G * d)]
    kv_plan += [(1, c0, w, 1, c0, "kv_group") for c0, w in _chunks(G * 4 * LANES, 4 * LANES)]

    shared = None
    for layer in range(depth):
        sh_a, sc_a, gt_a, sh_m, sc_m, gt_m = jnp.split(mod[layer], 6, axis=-1)
        if layer < n_a:
            w_in = a_w_in[layer].astype(BF16)
            plan = [(0, c0, w, 0, c0, "scale" if c0 < D else "plain") for c0, w in _chunks(3 * D)]
            (qkv,) = _proj(x, attn_norm[layer], sh_a, sc_a, [w_in], [(3 * D, BF16)], plan, "diff_qkv_proj")
            mix = _diff_attention(qkv, rel_bias, a_lambda[layer], a_subln[layer], layer)
            w_out = a_w_out[layer].astype(BF16)
        else:
            i = layer - n_a
            w_q = b_w_in[i][:, :D].astype(BF16)
            w_g = jnp.tile(b_w_in[i][:, D:].reshape(D, G, 3 * R), (1, 1, 3))
            w_g = jnp.pad(w_g, ((0, 0), (0, 0), (0, LANES - 9 * R))).reshape(D, G * LANES).astype(BF16)
            plan = [(0, c0, w, 0, 2 * c0, "scale_pad") for c0, w in _chunks(D)]
            plan += [(1, 0, G * LANES, 1, 0, "sigmoid")]
            q, gates = _proj(x, attn_norm[layer], sh_a, sc_a, [w_q, w_g], [(2 * D, BF16), (G * LANES, F32)], plan,
                             "nsa_in_proj")
            kc, kv_sw = shared
            o_cmp, sel = _nsa_compressed_select(q, kc)
            mix = _nsa_main(q, sel, kv_sw, gates, o_cmp, rel_bias)
            w_out = b_w_out[i].astype(BF16)
        x = _mix_mlp_residual(x, mix, w_out, gt_a, mlp_norm[layer], sh_m, sc_m, gt_m, mlp_w1[layer].astype(BF16),
                              mlp_w2[layer].astype(BF16), final_norm, final=(layer == depth - 1))
        if layer == n_a - 1:
            sh_kv, sc_kv = jnp.split(kv_mod, 2, axis=-1)
            kv_cmp, kv_sw = _proj(x, kv_norm, sh_kv, sc_kv, [w_cmp, w_sw], [(2 * G * d, BF16), (G * 4 * LANES, BF16)],
                                  kv_plan, "nsa_kv_proj")
            shared = (_compress(kv_cmp, cmp_pos, cmp_w1, cmp_w2), kv_sw)
    return x
```
